```python
import math
import jax
import jax.numpy as jnp
from jax import lax
import numpy as np

D_MODEL = 2048
BATCH = 2
SEQ = 4096
DEPTH = 4

GRID_W = 64
CTX_LEN = 256
EPS = 1e-6

GDN_HEADS = 6
GDN_DK = 128
GDN_DV = 128
GDN_W = GDN_HEADS * GDN_DV
GDN_CHUNK = 64
QKV_CONV = 3
SC_GROUPS = 4
SC_GROUP_DIM = 128
SC_W = SC_GROUPS * SC_GROUP_DIM
SC_CONV = 3
DIFF_HEADS = 6
DIFF_DH = 64
DIFF_DV = 2 * DIFF_DH
DIFF_W = DIFF_HEADS * DIFF_DV
Q_BLOCK = 128
ROPE_BASE = 10000.0
ROPE_PAIRS = DIFF_DH // 4

MIX_W = GDN_W + SC_W + DIFF_W
A_IN = 4 * GDN_W + 4 * GDN_HEADS
B_IN = 3 * SC_W
C_IN = 3 * DIFF_W
N_IN = A_IN + B_IN + C_IN

N_EXPERTS = 16
EXPERT_FF = 1024
EC_CAPACITY = 2

kernel_name = 'hybrid_gdn_shortconv_diffattn_ecmoe_dit'


def rmsnorm(x, g):
    xf = x.astype(jnp.float32)
    y = xf * lax.rsqrt(jnp.mean(xf * xf, -1, keepdims=True) + EPS)
    return (y * g.astype(jnp.float32)).astype(x.dtype)


def l2norm(x):
    xf = x.astype(jnp.float32)
    return xf * lax.rsqrt(jnp.sum(xf * xf, -1, keepdims=True) + EPS)


def modulate(h, shift, scale):
    return h * (1 + scale) + shift


def depthwise_conv(x, w):
    k = w.shape[0]
    return lax.conv_general_dilated(
        x, w[:, None, :].astype(x.dtype), window_strides=(1,),
        padding=[((k - 1) // 2, (k - 1) // 2)],
        dimension_numbers=('NWC', 'WIO', 'NWC'), feature_group_count=x.shape[-1])


def axial_rope_tables(t):
    rows = t // GRID_W
    row = jnp.repeat(jnp.arange(rows, dtype=jnp.float32), GRID_W)
    col = jnp.tile(jnp.arange(GRID_W, dtype=jnp.float32), rows)
    inv = ROPE_BASE ** (-jnp.arange(ROPE_PAIRS, dtype=jnp.float32) / ROPE_PAIRS)
    ang = jnp.stack([row[:, None] * inv, col[:, None] * inv], 1)
    return jnp.cos(ang), jnp.sin(ang)


def axial_rope(x, cos, sin):
    shp = x.shape
    xr = x.astype(jnp.float32).reshape(shp[:-1] + (2, 2, ROPE_PAIRS))
    x1, x2 = xr[..., 0, :], xr[..., 1, :]
    c = cos[None, :, None, None]
    s = sin[None, :, None, None]
    y = jnp.stack([x1 * c - x2 * s, x2 * c + x1 * s], -2)
    return y.reshape(shp).astype(x.dtype)


def gdn_chunked(q, k, v, g, beta, s0):
    bsz, t = q.shape[:2]
    n = t // GDN_CHUNK
    dv = v.shape[-1]

    def to_chunks(a):
        return jnp.moveaxis(a.reshape((bsz, n, GDN_CHUNK) + a.shape[2:]), 3, 1)

    q, k, v, g, beta = (to_chunks(a) for a in (q, k, v, g, beta))
    gc = jnp.cumsum(g, -1)
    idx = jnp.arange(GDN_CHUNK)
    incl = idx[:, None] >= idx[None, :]
    strict = idx[:, None] > idx[None, :]
    decay = jnp.exp(jnp.where(incl, gc[..., :, None] - gc[..., None, :], -jnp.inf))
    kb = k * beta[..., None]
    low = jnp.where(strict, jnp.einsum('bhnid,bhnjd->bhnij', kb, k) * decay, 0.0)
    a_mat = low + jnp.eye(GDN_CHUNK, dtype=low.dtype)
    rhs = jnp.concatenate([v * beta[..., None], kb * jnp.exp(gc)[..., None]], -1)
    sol = lax.linalg.triangular_solve(a_mat, rhs, left_side=True, lower=True, unit_diagonal=True)
    u, w = sol[..., :dv], sol[..., dv:]
    attn = jnp.einsum('bhnid,bhnjd->bhnij', q, k) * decay
    q_dec = q * jnp.exp(gc)[..., None]
    k_dec = k * jnp.exp(gc[..., -1:] - gc)[..., None]
    g_last = jnp.exp(gc[..., -1])
    xs = tuple(jnp.moveaxis(a, 2, 0) for a in (u, w, attn, q_dec, k_dec, g_last))

    def step(s, inp):
        u_n, w_n, a_n, qd_n, kd_n, gl_n = inp
        v_new = u_n - jnp.einsum('bhcd,bhde->bhce', w_n, s)
        o = jnp.einsum('bhcd,bhde->bhce', qd_n, s) + jnp.einsum('bhij,bhje->bhie', a_n, v_new)
        s = s * gl_n[..., None, None] + jnp.einsum('bhcd,bhce->bhde', kd_n, v_new)
        return s, o

    s_final, o = lax.scan(step, s0, xs)
    o = jnp.moveaxis(jnp.moveaxis(o, 0, 2), 1, 3).reshape(bsz, t, GDN_HEADS, dv)
    return o, s_final


def gdn_mixer(uc, ux, conv_w, a_log, dt_bias, norm_g, with_ctx):
    def prep(u):
        bsz, t = u.shape[:2]
        qkv, z, b, a = jnp.split(u, [3 * GDN_W, 4 * GDN_W, 4 * GDN_W + 2 * GDN_HEADS], -1)
        qkv = jax.nn.silu(depthwise_conv(qkv, conv_w))
        q, k, v = (m.reshape(bsz, t, GDN_HEADS, -1) for m in jnp.split(qkv, 3, -1))
        q = l2norm(q) * GDN_DK ** -0.5
        k = l2norm(k)
        v = v.astype(jnp.float32)
        beta = jax.nn.sigmoid(b.astype(jnp.float32)).reshape(bsz, t, 2, GDN_HEADS)
        g = -jnp.exp(a_log.astype(jnp.float32)) * jax.nn.softplus(
            a.astype(jnp.float32).reshape(bsz, t, 2, GDN_HEADS) + dt_bias.astype(jnp.float32))
        return q, k, v, g, beta, z

    qc, kc, vc, gcx, bc, zc = prep(uc)
    qx, kx, vx, gx, bx, zx = prep(ux)
    s0 = jnp.zeros((ux.shape[0], GDN_HEADS, GDN_DK, GDN_DV), jnp.float32)
    oc = 0.0
    ox = 0.0
    for d in range(2):
        def rev(a):
            return jnp.flip(a, 1) if d == 1 else a
        o_c, s_ctx = gdn_chunked(rev(qc), rev(kc), rev(vc), rev(gcx[:, :, d]), rev(bc[:, :, d]), s0)
        o_x, _ = gdn_chunked(rev(qx), rev(kx), rev(vx), rev(gx[:, :, d]), rev(bx[:, :, d]), s_ctx)
        oc = oc + rev(o_c)
        ox = ox + rev(o_x)

    def out(o, z):
        bsz, t = z.shape[:2]
        y = rmsnorm(o, norm_g) * jax.nn.silu(z.astype(jnp.float32)).reshape(bsz, t, GDN_HEADS, GDN_DV)
        return y.reshape(bsz, t, GDN_W).astype(z.dtype)

    return (out(oc, zc) if with_ctx else None), out(ox, zx)


def short_conv_mixer(u, conv_w):
    b, cg, xin = jnp.split(u, 3, -1)
    return b * depthwise_conv(cg * xin, conv_w)


def diff_attend(q, k, v, lam):
    s = jnp.einsum('bqhmd,bkhmd->bhmqk', q, k).astype(jnp.float32) * DIFF_DH ** -0.5
    p = jax.nn.softmax(s, -1)
    wts = p[:, :, 0] - lam * p[:, :, 1]
    return jnp.einsum('bhqk,bkhd->bqhd', wts.astype(v.dtype), v)


def diff_attn_mixer(uc, ux, lam_params, norm_g, lam_init, cos, sin, with_ctx):
    def heads(u):
        bsz, t = u.shape[:2]
        q, k, v = jnp.split(u, 3, -1)
        return (q.reshape(bsz, t, DIFF_HEADS, 2, DIFF_DH), k.reshape(bsz, t, DIFF_HEADS, 2, DIFF_DH),
                v.reshape(bsz, t, DIFF_HEADS, DIFF_DV))

    qc, kc, vc = heads(uc)
    qx, kx, vx = heads(ux)
    qx = axial_rope(qx, cos, sin)
    kx = axial_rope(kx, cos, sin)
    lp = lam_params.astype(jnp.float32)
    lam = jnp.exp(jnp.sum(lp[0] * lp[1])) - jnp.exp(jnp.sum(lp[2] * lp[3])) + lam_init
    k_all = jnp.concatenate([kc, kx], 1)
    v_all = jnp.concatenate([vc, vx], 1)
    bsz, t = ux.shape[:2]
    nb = t // Q_BLOCK
    q_blocks = jnp.swapaxes(qx.reshape(bsz, nb, Q_BLOCK, DIFF_HEADS, 2, DIFF_DH), 0, 1)
    o_blocks = lax.map(lambda qb: diff_attend(qb, k_all, v_all, lam), q_blocks)
    ox = jnp.swapaxes(o_blocks, 0, 1).reshape(bsz, t, DIFF_HEADS, DIFF_DV)

    def out(o):
        return (rmsnorm(o, norm_g) * (1.0 - lam_init)).reshape(o.shape[0], o.shape[1], DIFF_W)

    oc = out(diff_attend(qc, kc, vc, lam)) if with_ctx else None
    return oc, out(ox)


def expert_choice_ffn(h, w_router, w_gate, w_up, w_down):
    bsz, t, d = h.shape
    cap = EC_CAPACITY * t // N_EXPERTS
    aff = jax.nn.softmax((h @ w_router).astype(jnp.float32), -1)
    gate, idx = lax.top_k(jnp.swapaxes(aff, 1, 2), cap)
    xs = jax.vmap(lambda hb, ib: hb[ib])(h, idx)
    hid = jax.nn.silu(jnp.einsum('becd,edf->becf', xs, w_gate)) * jnp.einsum('becd,edf->becf', xs, w_up)
    y = jnp.einsum('becf,efd->becd', hid, w_down) * gate[..., None].astype(h.dtype)
    return jax.vmap(lambda ib, yb: jnp.zeros((t, d), yb.dtype).at[ib.reshape(-1)].add(yb.reshape(-1, d)))(idx, y)


def setup_inputs(seed: int = 0) -> dict:
    key = jax.random.key(seed)
    ks = jax.random.split(key, 24)
    f32 = jnp.float32

    def nrm(k, shape, scale):
        return jax.random.normal(k, shape, f32) * scale

    dt = jnp.exp(jax.random.uniform(ks[11], (DEPTH, 2, GDN_HEADS), f32, math.log(1e-3), math.log(1e-1)))
    return {
        'x': nrm(ks[0], (BATCH, SEQ, D_MODEL), 1.0),
        'c': nrm(ks[1], (BATCH, D_MODEL), 1.0),
        'ctx': nrm(ks[2], (BATCH, CTX_LEN, D_MODEL), 1.0),
        'c_ctx': nrm(ks[3], (D_MODEL,), 1.0),
        'w_mod': nrm(ks[4], (DEPTH, D_MODEL, 6 * D_MODEL), 0.5 * D_MODEL ** -0.5),
        'b_mod': nrm(ks[5], (DEPTH, 6 * D_MODEL), 0.02),
        'norm1_g': 1.0 + nrm(ks[6], (DEPTH, D_MODEL), 0.02),
        'norm2_g': 1.0 + nrm(ks[7], (DEPTH, D_MODEL), 0.02),
        'w_in': nrm(ks[8], (DEPTH, D_MODEL, N_IN), D_MODEL ** -0.5),
        'gdn_conv_w': nrm(ks[9], (DEPTH, QKV_CONV, 3 * GDN_W), QKV_CONV ** -0.5),
        'gdn_a_log': jnp.log(jax.random.uniform(ks[10], (DEPTH, 2, GDN_HEADS), f32, 1.0, 16.0)),
        'gdn_dt_bias': dt + jnp.log(-jnp.expm1(-dt)),
        'gdn_norm_g': 1.0 + nrm(ks[12], (DEPTH, GDN_DV), 0.02),
        'sc_conv_w': nrm(ks[13], (DEPTH, SC_CONV, SC_W), SC_CONV ** -0.5),
        'diff_lambda': nrm(ks[14], (DEPTH, 4, DIFF_DH), 0.1),
        'diff_norm_g': 1.0 + nrm(ks[15], (DEPTH, DIFF_DV), 0.02),
        'w_out': nrm(ks[16], (DEPTH, MIX_W, D_MODEL), MIX_W ** -0.5),
        'w_router': nrm(ks[17], (DEPTH, D_MODEL, N_EXPERTS), D_MODEL ** -0.5),
        'w_e_gate': nrm(ks[18], (DEPTH, N_EXPERTS, D_MODEL, EXPERT_FF), D_MODEL ** -0.5),
        'w_e_up': nrm(ks[19], (DEPTH, N_EXPERTS, D_MODEL, EXPERT_FF), D_MODEL ** -0.5),
        'w_e_down': nrm(ks[20], (DEPTH, N_EXPERTS, EXPERT_FF, D_MODEL), EXPERT_FF ** -0.5),
        'final_norm_g': 1.0 + nrm(ks[21], (D_MODEL,), 0.02),
    }


def reference(x, c, ctx, c_ctx, w_mod, b_mod, norm1_g, norm2_g, w_in, gdn_conv_w, gdn_a_log,
              gdn_dt_bias, gdn_norm_g, sc_conv_w, diff_lambda, diff_norm_g, w_out, w_router,
              w_e_gate, w_e_up, w_e_down, final_norm_g):
    t = x.shape[1]
    tc = ctx.shape[1]
    cos, sin = axial_rope_tables(t)
    sc_x = jax.nn.silu(c)
    sc_c = jax.nn.silu(c_ctx)
    hx, hc = x, ctx
    for l in range(DEPTH):
        with_ctx = l < DEPTH - 1
        lam_init = 0.8 - 0.6 * math.exp(-0.3 * l)
        mx = jnp.split((sc_x @ w_mod[l] + b_mod[l])[:, None, :], 6, -1)
        mc = jnp.split(sc_c @ w_mod[l] + b_mod[l], 6, -1)
        ax = modulate(rmsnorm(hx, norm1_g[l]), mx[0], mx[1])
        ac = modulate(rmsnorm(hc, norm1_g[l]), mc[0], mc[1])
        u = jnp.concatenate([ac, ax], 1) @ w_in[l]
        uc, ux = u[:, :tc], u[:, tc:]
        uc_a, uc_b, uc_c = jnp.split(uc, [A_IN, A_IN + B_IN], -1)
        ux_a, ux_b, ux_c = jnp.split(ux, [A_IN, A_IN + B_IN], -1)
        oc_a, ox_a = gdn_mixer(uc_a, ux_a, gdn_conv_w[l], gdn_a_log[l], gdn_dt_bias[l], gdn_norm_g[l], with_ctx)
        ox_b = short_conv_mixer(ux_b, sc_conv_w[l])
        oc_c, ox_c = diff_attn_mixer(uc_c, ux_c, diff_lambda[l], diff_norm_g[l], lam_init, cos, sin, with_ctx)
        hx = hx + mx[2] * (jnp.concatenate([ox_a, ox_b, ox_c], -1) @ w_out[l])
        hx = hx + mx[5] * expert_choice_ffn(modulate(rmsnorm(hx, norm2_g[l]), mx[3], mx[4]),
                                            w_router[l], w_e_gate[l], w_e_up[l], w_e_down[l])
        if with_ctx:
            oc_b = short_conv_mixer(uc_b, sc_conv_w[l])
            hc = hc + mc[2] * (jnp.concatenate([oc_a, oc_b, oc_c], -1) @ w_out[l])
            hc = hc + mc[5] * expert_choice_ffn(modulate(rmsnorm(hc, norm2_g[l]), mc[3], mc[4]),
                                                w_router[l], w_e_gate[l], w_e_up[l], w_e_down[l])
    return rmsnorm(hx, final_norm_g)
```

```python
import functools
import math

import jax
import jax.numpy as jnp
from jax import lax
from jax.experimental import pallas as pl
from jax.experimental.pallas import tpu as pltpu

F32 = jnp.float32
BF16 = jnp.bfloat16

EPS = 1e-6
GRID_W = 64
GDN_HEADS = 6
GDN_D = 128
GDN_W = GDN_HEADS * GDN_D
SC_W = 512
DIFF_HEADS = 6
DIFF_DH = 64
DIFF_DV = 128
DIFF_W = DIFF_HEADS * DIFF_DV
ROPE_BASE = 10000.0
ROPE_PAIRS = DIFF_DH // 4
N_EXPERTS = 16
EC_CAPACITY = 2

COL_Z = 3 * GDN_W
COL_SC = COL_Z + GDN_W
COL_ATT = COL_SC + 3 * SC_W
N_MAIN = COL_ATT + 3 * DIFF_W
LANE = 128
ROW_TILE = 256
GDN_CHUNK = 128
VMEM_LIMIT = 56 * 1024 * 1024


def _cparams(sem, vmem=VMEM_LIMIT):
    return pltpu.CompilerParams(dimension_semantics=sem, vmem_limit_bytes=vmem)


def _sigmoid(x):
    return 1.0 / (1.0 + jnp.exp(-x))


def _silu(x):
    return x * _sigmoid(x)


def _softplus(x):
    return jnp.maximum(x, 0.0) + jnp.log(1.0 + jnp.exp(-jnp.abs(x)))


def _rms(x, g):
    return x * lax.rsqrt(jnp.mean(x * x, axis=-1, keepdims=True) + EPS) * g


def _dot(a, b, **kw):
    return jnp.dot(a, b, preferred_element_type=F32, **kw)


def _dot_nt(a, b):
    return lax.dot_general(a, b, (((1,), (1,)), ((), ())), preferred_element_type=F32)


def _mod_kernel(s_ref, w_ref, b_ref, o_ref):
    s = _silu(s_ref[...]).astype(BF16)
    o_ref[0] = _dot(s, w_ref[0].astype(BF16)) + b_ref[0]


def _modulation(cond, w_mod, b_mod):
    depth, d, n = w_mod.shape
    tn = 1536
    return pl.pallas_call(
        _mod_kernel,
        grid=(depth, n // tn),
        in_specs=[pl.BlockSpec((8, d), lambda l, j: (0, 0)),
                  pl.BlockSpec((1, d, tn), lambda l, j: (l, 0, j)),
                  pl.BlockSpec((1, 1, tn), lambda l, j: (l, 0, j))],
        out_specs=pl.BlockSpec((1, 8, tn), lambda l, j: (l, 0, j)),
        out_shape=jax.ShapeDtypeStruct((depth, 8, n), F32),
        compiler_params=_cparams(("arbitrary", "arbitrary")),
        name="modulation",
    )(cond, w_mod, b_mod.reshape(depth, 1, n))


def _mod_row_map(tiles_per_sample):
    def index_map(i):
        return (jnp.where(i % tiles_per_sample == 0, 2, i // tiles_per_sample), 0, 0)
    return index_map


def _norm_mod_kernel(h_ref, g_ref, m_ref, a_ref):
    m = m_ref[0]
    a = _rms(h_ref[...], g_ref[...]) * (1.0 + m[1:2]) + m[0:1]
    a_ref[...] = a.astype(a_ref.dtype)


def _norm_mod(h, g, mods, tiles_per_sample):
    r, d = h.shape
    return pl.pallas_call(
        _norm_mod_kernel,
        grid=(r // ROW_TILE,),
        in_specs=[pl.BlockSpec((ROW_TILE, d), lambda i: (i, 0)),
                  pl.BlockSpec((1, d), lambda i: (0, 0)),
                  pl.BlockSpec((1, 6, d), _mod_row_map(tiles_per_sample))],
        out_specs=pl.BlockSpec((ROW_TILE, d), lambda i: (i, 0)),
        out_shape=jax.ShapeDtypeStruct((r, d), BF16),
        compiler_params=_cparams(("arbitrary",)),
        name="norm_mod",
    )(h, g.reshape(1, d), mods)


def _mm_kernel(a_ref, w_ref, o_ref):
    o_ref[...] = _dot(a_ref[...], w_ref[...]).astype(o_ref.dtype)


def _matmul(a, w, out_dtype, tm, tn):
    r, k = a.shape
    n = w.shape[1]
    return pl.pallas_call(
        _mm_kernel,
        grid=(n // tn, r // tm),
        in_specs=[pl.BlockSpec((tm, k), lambda j, i: (i, 0)),
                  pl.BlockSpec((k, tn), lambda j, i: (0, j))],
        out_specs=pl.BlockSpec((tm, tn), lambda j, i: (i, j)),
        out_shape=jax.ShapeDtypeStruct((r, n), out_dtype),
        compiler_params=_cparams(("arbitrary", "arbitrary")),
        name="in_proj",
    )(a, w)


def _conv3(x, w, tc):
    l = x.shape[0]
    row = lax.broadcasted_iota(jnp.int32, x.shape, 0)
    prev = jnp.where((row == 0) | (row == tc), 0.0, pltpu.roll(x, 1, 0))
    nxt = jnp.where((row == tc - 1) | (row == l - 1), 0.0, pltpu.roll(x, l - 1, 0))
    return prev * w[0:1] + x * w[1:2] + nxt * w[2:3]


def _gdn_prep_kernel(u_ref, w_ref, o_ref, *, tc):
    j = pl.program_id(1)
    y = _silu(_conv3(u_ref[0].astype(F32), w_ref[...], tc))
    inv = lax.rsqrt(jnp.sum(y * y, axis=-1, keepdims=True) + EPS)
    nh = GDN_HEADS
    fac = jnp.where(j < nh, inv * GDN_D ** -0.5, jnp.where(j < 2 * nh, inv, 1.0))
    o_ref[0] = (y * fac).astype(o_ref.dtype)


def _gdn_prep(u3, conv_w, tc):
    b, l, _ = u3.shape
    nblk = 3 * GDN_W // LANE
    return pl.pallas_call(
        functools.partial(_gdn_prep_kernel, tc=tc),
        grid=(b, nblk),
        in_specs=[pl.BlockSpec((1, l, LANE), lambda bi, j: (bi, 0, j)),
                  pl.BlockSpec((3, LANE), lambda bi, j: (0, j))],
        out_specs=pl.BlockSpec((1, l, LANE), lambda bi, j: (bi, 0, j)),
        out_shape=jax.ShapeDtypeStruct((b, l, 3 * GDN_W), BF16),
        compiler_params=_cparams(("arbitrary", "arbitrary")),
        name="gdn_prep",
    )(u3, conv_w)


def _sconv_kernel(b_ref, c_ref, x_ref, w_ref, o_ref, *, tc):
    inner = c_ref[0].astype(F32) * x_ref[0].astype(F32)
    o_ref[0] = (b_ref[0].astype(F32) * _conv3(inner, w_ref[...], tc)).astype(o_ref.dtype)


def _short_conv(u3, conv_w, tc):
    b, l, _ = u3.shape
    nblk = SC_W // LANE
    base = COL_SC // LANE

    def spec(k):
        return pl.BlockSpec((1, l, LANE), lambda bi, j: (bi, 0, base + k * nblk + j))

    return pl.pallas_call(
        functools.partial(_sconv_kernel, tc=tc),
        grid=(b, nblk),
        in_specs=[spec(0), spec(1), spec(2), pl.BlockSpec((3, LANE), lambda bi, j: (0, j))],
        out_specs=pl.BlockSpec((1, l, LANE), lambda bi, j: (bi, 0, j)),
        out_shape=jax.ShapeDtypeStruct((b, l, SC_W), BF16),
        compiler_params=_cparams(("arbitrary", "arbitrary")),
        name="short_conv",
    )(u3, u3, u3, conv_w)


def _rope_kernel(u_ref, cos_ref, sin_ref, o_ref):
    j = pl.program_id(1)
    x = u_ref[0].astype(F32)
    lane = lax.broadcasted_iota(jnp.int32, x.shape, 1)
    swapped = jnp.where(lane % 32 < 16, pltpu.roll(x, LANE - 16, 1), pltpu.roll(x, 16, 1))
    y = x * cos_ref[...] + swapped * sin_ref[...]
    scale = jnp.where(j < DIFF_HEADS, DIFF_DH ** -0.5, 1.0)
    o_ref[0] = (y * scale).astype(o_ref.dtype)


def _rope_prep(u3, cos_t, sin_t):
    b, l, _ = u3.shape
    nblk = 2 * DIFF_W // LANE
    base = COL_ATT // LANE
    return pl.pallas_call(
        _rope_kernel,
        grid=(b, nblk),
        in_specs=[pl.BlockSpec((1, l, LANE), lambda bi, j: (bi, 0, base + j)),
                  pl.BlockSpec((l, LANE), lambda bi, j: (0, 0)),
                  pl.BlockSpec((l, LANE), lambda bi, j: (0, 0))],
        out_specs=pl.BlockSpec((1, l, LANE), lambda bi, j: (bi, 0, j)),
        out_shape=jax.ShapeDtypeStruct((b, l, 2 * DIFF_W), BF16),
        compiler_params=_cparams(("arbitrary", "arbitrary")),
        name="rope_prep",
    )(u3, cos_t, sin_t)


def _rope_tables(t, tc):
    rows = t // GRID_W
    row = jnp.repeat(jnp.arange(rows, dtype=F32), GRID_W)
    col = jnp.tile(jnp.arange(GRID_W, dtype=F32), rows)
    inv = ROPE_BASE ** (-jnp.arange(ROPE_PAIRS, dtype=F32) / ROPE_PAIRS)
    lane = jnp.arange(LANE)
    axis = (lane // 32) % 2
    pos = jnp.where(axis[None, :] == 0, row[:, None], col[:, None])
    ang = pos * inv[lane % 16][None, :]
    sign = jnp.where(lane % 32 < 16, -1.0, 1.0)[None, :]
    cos_t = jnp.concatenate([jnp.ones((tc, LANE), F32), jnp.cos(ang)], 0)
    sin_t = jnp.concatenate([jnp.zeros((tc, LANE), F32), jnp.sin(ang) * sign], 0)
    return cos_t, sin_t


def _gdn_kernel(qf_ref, kf_ref, vf_ref, gf_ref, qb_ref, kb_ref, vb_ref, gb_ref, prm_ref,
                of_ref, ob_ref, s_ref, *, chunk):
    c = chunk
    nh = GDN_HEADS

    @pl.when(pl.program_id(1) == 0)
    def _():
        s_ref[...] = jnp.zeros_like(s_ref)

    ii = lax.broadcasted_iota(jnp.int32, (c, c), 0)
    jj = lax.broadcasted_iota(jnp.int32, (c, c), 1)
    neg_a = prm_ref[0:1, :]
    dt_b = prm_ref[1:2, :]
    blk = [(ii >> sh) == (jj >> sh) for sh in range(3, int(math.log2(c)) + 1)]
    dirs = ((qf_ref, kf_ref, vf_ref, gf_ref, of_ref), (qb_ref, kb_ref, vb_ref, gb_ref, ob_ref))
    for d, (q_ref, k_ref, v_ref, g_ref, o_ref) in enumerate(dirs):
        incl = (ii >= jj) if d == 0 else (ii <= jj)
        strict = (ii > jj) if d == 0 else (ii < jj)
        raw = g_ref[0]
        g_all = neg_a * _softplus(raw + dt_b)
        beta_all = _sigmoid(raw)
        tri = incl.astype(F32)
        cum = _dot(tri, g_all, precision=lax.Precision.HIGHEST)
        cum_t = _dot(g_all.T, tri.T, precision=lax.Precision.HIGHEST)
        last = c - 1 if d == 0 else 0
        for h in range(nh):
            cb = d * nh + h
            cg = 2 * nh + cb
            sl = slice(h * GDN_D, (h + 1) * GDN_D)
            beta = beta_all[:, cb:cb + 1]
            cc = cum[:, cg:cg + 1]
            cr = cum_t[cg:cg + 1, :]
            tot = cum[last:last + 1, cg:cg + 1]
            decay = jnp.exp(jnp.where(incl, cc - cr, -1e30))
            q = q_ref[0, :, sl]
            k = k_ref[0, :, sl]
            v = v_ref[0, :, sl]
            kf = k.astype(F32)
            kbeta = kf * beta
            gram = _dot_nt(jnp.concatenate([kbeta.astype(BF16), q], axis=0), k)
            low = jnp.where(strict, gram[:c] * decay, 0.0)
            attn = gram[c:] * decay
            p = jnp.where(blk[0], -low, 0.0)
            n = p
            for _ in range(2):
                pb = p.astype(BF16)
                p = _dot(pb, pb)
                n = n + p + _dot(n.astype(BF16), p.astype(BF16))
            for lvl in range(1, len(blk)):
                off = jnp.where(blk[lvl] & ~blk[lvl - 1], low, 0.0).astype(BF16)
                m1 = off + _dot(off, n.astype(BF16))
                n = n - m1 - _dot(n.astype(BF16), m1.astype(BF16))
            eg = jnp.exp(cc)
            rhs = jnp.concatenate([v.astype(F32) * beta, kbeta * eg], axis=1)
            sol = rhs + _dot(n.astype(BF16), rhs.astype(BF16))
            u = sol[:, :GDN_D]
            w = sol[:, GDN_D:]
            qd = q.astype(F32) * eg
            kd = kf * jnp.exp(tot - cc)
            s = s_ref[cb]
            ws = _dot(jnp.concatenate([w.astype(BF16), qd.astype(BF16)], axis=0), s.astype(BF16))
            v_new = (u - ws[:c]).astype(BF16)
            o_ref[0, :, sl] = ws[c:] + _dot(attn.astype(BF16), v_new)
            s_ref[cb] = s * jnp.exp(tot) + _dot(kd.T.astype(BF16), v_new)


def _gdn(qkv, ba, prm, tc, chunk):
    b, l, _ = qkv.shape
    nc = l // chunk
    ncc = tc // chunk

    def fwd(col):
        return lambda bi, n: (bi, n, col)

    def bwd(col):
        return lambda bi, n: (bi, jnp.where(n < ncc, ncc - 1 - n, nc - 1 + ncc - n), col)

    blk = (1, chunk, GDN_W)
    gblk = (1, chunk, LANE)
    out = jax.ShapeDtypeStruct((b, l, GDN_W), F32)
    return pl.pallas_call(
        functools.partial(_gdn_kernel, chunk=chunk),
        grid=(b, nc),
        in_specs=[pl.BlockSpec(blk, fwd(0)), pl.BlockSpec(blk, fwd(1)), pl.BlockSpec(blk, fwd(2)),
                  pl.BlockSpec(gblk, fwd(0)),
                  pl.BlockSpec(blk, bwd(0)), pl.BlockSpec(blk, bwd(1)), pl.BlockSpec(blk, bwd(2)),
                  pl.BlockSpec(gblk, bwd(0)),
                  pl.BlockSpec((8, LANE), lambda bi, n: (0, 0))],
        out_specs=[pl.BlockSpec(blk, fwd(0)), pl.BlockSpec(blk, bwd(0))],
        out_shape=[out, out],
        scratch_shapes=[pltpu.VMEM((2 * GDN_HEADS, GDN_D, GDN_D), F32)],
        compiler_params=_cparams(("arbitrary", "arbitrary")),
        name="gdn",
    )(qkv, qkv, qkv, ba, qkv, qkv, qkv, ba, prm)


def _attn_kernel(q_ref, k_ref, v_ref, lp_ref, g_ref, o_ref, m_sc, l_sc, acc_sc,
                 *, tq, tk, nk_ctx, nk_all, lam_init, q_off):
    i = pl.program_id(2) + q_off
    q = q_ref[0]
    lane = lax.broadcasted_iota(jnp.int32, q.shape, 1)
    zero = jnp.zeros_like(q)
    qq = jnp.concatenate([jnp.where(lane < DIFF_DH, q, zero), jnp.where(lane >= DIFF_DH, q, zero)], axis=0)
    m_sc[...] = jnp.full_like(m_sc, -jnp.inf)
    l_sc[...] = jnp.zeros_like(l_sc)
    acc_sc[...] = jnp.zeros_like(acc_sc)

    def body(c, carry):
        off = pl.multiple_of(c * tk, tk)
        kc = k_ref[0, pl.ds(off, tk), :]
        vc = v_ref[0, pl.ds(off, tk), :]
        s = _dot_nt(qq, kc)
        m_old = m_sc[...]
        m_new = jnp.maximum(m_old, jnp.max(s, axis=1, keepdims=True))
        p = jnp.exp(s - m_new)
        alpha = jnp.exp(m_old - m_new)
        l_sc[...] = alpha * l_sc[...] + jnp.sum(p, axis=1, keepdims=True)
        acc_sc[...] = alpha * acc_sc[...] + _dot(p.astype(BF16), vc)
        m_sc[...] = m_new
        return carry

    lax.fori_loop(0, jnp.where(i == 0, nk_ctx, nk_all), body, 0)
    lp = lp_ref[...]
    lam = (jnp.exp(jnp.sum(lp[0:1] * lp[1:2], axis=1, keepdims=True))
           - jnp.exp(jnp.sum(lp[2:3] * lp[3:4], axis=1, keepdims=True)) + lam_init)
    on = acc_sc[...] / l_sc[...]
    o = on[:tq] - lam * on[tq:]
    o_ref[0] = (_rms(o, g_ref[...]) * (1.0 - lam_init)).astype(o_ref.dtype)


def _diff_attn(qk, u3, lam_params, norm_g, lam_init, tc):
    b, l, _ = qk.shape
    tq = ROW_TILE
    tk = 256
    q_off = 0
    nq = l // tq - q_off
    vbase = (COL_ATT + 2 * DIFF_W) // LANE
    kern = functools.partial(_attn_kernel, tq=tq, tk=tk, nk_ctx=tc // tk, nk_all=l // tk,
                             lam_init=lam_init, q_off=q_off)
    return pl.pallas_call(
        kern,
        grid=(b, DIFF_HEADS, nq),
        in_specs=[pl.BlockSpec((1, tq, LANE), lambda bi, h, i: (bi, i + q_off, h)),
                  pl.BlockSpec((1, l, LANE), lambda bi, h, i: (bi, 0, DIFF_HEADS + h)),
                  pl.BlockSpec((1, l, LANE), lambda bi, h, i: (bi, 0, vbase + h)),
                  pl.BlockSpec((4, DIFF_DH), lambda bi, h, i: (0, 0)),
                  pl.BlockSpec((1, DIFF_DV), lambda bi, h, i: (0, 0))],
        out_specs=pl.BlockSpec((1, tq, LANE), lambda bi, h, i: (bi, i + q_off, h)),
        out_shape=jax.ShapeDtypeStruct((b, l, DIFF_W), BF16),
        scratch_shapes=[pltpu.VMEM((2 * tq, 1), F32), pltpu.VMEM((2 * tq, 1), F32),
                        pltpu.VMEM((2 * tq, DIFF_DV), F32)],
        compiler_params=_cparams(("arbitrary", "arbitrary", "arbitrary")),
        name="diff_attn",
    )(qk, qk, u3, lam_params, norm_g.reshape(1, DIFF_DV))


def _outproj_kernel(of_ref, ob_ref, z_ref, yb_ref, yc_ref, w_ref, h_ref, m_ref, gn_ref, g2_ref, wr_ref,
                    h_out, a_out, aff_out):
    o = of_ref[...] + ob_ref[...]
    gn = gn_ref[...]
    parts = []
    for hd in range(GDN_HEADS):
        sl = slice(hd * GDN_D, (hd + 1) * GDN_D)
        parts.append((_rms(o[:, sl], gn) * _silu(z_ref[:, sl].astype(F32))).astype(BF16))
    ya = jnp.concatenate(parts, axis=1)
    acc = _dot(ya, w_ref[0:GDN_W, :])
    acc += _dot(yb_ref[...], w_ref[GDN_W:GDN_W + SC_W, :])
    acc += _dot(yc_ref[...], w_ref[GDN_W + SC_W:, :])
    m = m_ref[0]
    hn = h_ref[...] + m[2:3] * acc
    h_out[...] = hn
    a = _rms(hn, g2_ref[...]) * (1.0 + m[4:5]) + m[3:4]
    a_out[...] = a
    logits = _dot(a, wr_ref[...], precision=lax.Precision.HIGHEST)
    e = jnp.exp(logits - jnp.max(logits, axis=-1, keepdims=True))
    aff_out[...] = e / jnp.sum(e, axis=-1, keepdims=True)


def _outproj(o_f, o_b, u, y_b, y_c, w_out, h, mods, gdn_g, g2, w_router, tiles_per_sample):
    r, d = h.shape
    tm = ROW_TILE
    zb = COL_Z // GDN_W
    row = lambda i: (i, 0)
    const = lambda i: (0, 0)
    return pl.pallas_call(
        _outproj_kernel,
        grid=(r // tm,),
        in_specs=[pl.BlockSpec((tm, GDN_W), row), pl.BlockSpec((tm, GDN_W), row),
                  pl.BlockSpec((tm, GDN_W), lambda i: (i, zb)),
                  pl.BlockSpec((tm, SC_W), row), pl.BlockSpec((tm, DIFF_W), row),
                  pl.BlockSpec((d, d), const), pl.BlockSpec((tm, d), row),
                  pl.BlockSpec((1, 6, d), _mod_row_map(tiles_per_sample)),
                  pl.BlockSpec((1, GDN_D), const), pl.BlockSpec((1, d), const),
                  pl.BlockSpec((d, N_EXPERTS), const)],
        out_specs=[pl.BlockSpec((tm, d), row), pl.BlockSpec((tm, d), row),
                   pl.BlockSpec((tm, N_EXPERTS), row)],
        out_shape=[jax.ShapeDtypeStruct((r, d), F32), jax.ShapeDtypeStruct((r, d), F32),
                   jax.ShapeDtypeStruct((r, N_EXPERTS), F32)],
        compiler_params=_cparams(("arbitrary",)),
        name="out_proj",
    )(o_f, o_b, u, y_b, y_c, w_out, h, mods, gdn_g.reshape(1, GDN_D), g2.reshape(1, d), w_router)


def _moe_kernel(idx_ref, a_hbm, gate_ref, wg_ref, wu_ref, wd_ref, y_ref, x32, xb, sem, *, nrows, nf):
    e = pl.program_id(0)
    f = pl.program_id(1)

    def row_copy(src_row, r):
        return pltpu.make_async_copy(a_hbm.at[pl.ds(src_row, 1), :], x32.at[pl.ds(r, 1), :], sem.at[0])

    @pl.when(f == 0)
    def _():
        def issue(r, carry):
            row_copy(idx_ref[e, r], r).start()
            return carry

        def drain(r, carry):
            row_copy(0, r).wait()
            return carry

        lax.fori_loop(0, nrows, issue, 0)
        lax.fori_loop(0, nrows, drain, 0)
        xb[...] = x32[...].astype(BF16)

    x = xb[...]
    hid = _silu(_dot(x, wg_ref[0].astype(BF16))) * _dot(x, wu_ref[0].astype(BF16))
    part = _dot(hid.astype(BF16), wd_ref[0].astype(BF16))

    @pl.when(f == 0)
    def _():
        y_ref[0] = part

    @pl.when(f > 0)
    def _():
        y_ref[0] += part

    @pl.when(f == nf - 1)
    def _():
        y_ref[0] = y_ref[0] * gate_ref[0]


def _moe(idx, a, gate, w_gate, w_up, w_down):
    ne, nrows = idx.shape
    d = a.shape[1]
    ff = w_gate.shape[2]
    tf = 256
    nf = ff // tf
    grid_spec = pltpu.PrefetchScalarGridSpec(
        num_scalar_prefetch=1,
        grid=(ne, nf),
        in_specs=[pl.BlockSpec(memory_space=pl.ANY),
                  pl.BlockSpec((1, nrows, 1), lambda e, f, idx_ref: (e, 0, 0)),
                  pl.BlockSpec((1, d, tf), lambda e, f, idx_ref: (e, 0, f)),
                  pl.BlockSpec((1, d, tf), lambda e, f, idx_ref: (e, 0, f)),
                  pl.BlockSpec((1, tf, d), lambda e, f, idx_ref: (e, f, 0))],
        out_specs=pl.BlockSpec((1, nrows, d), lambda e, f, idx_ref: (e, 0, 0)),
        scratch_shapes=[pltpu.VMEM((nrows, d), F32), pltpu.VMEM((nrows, d), BF16),
                        pltpu.SemaphoreType.DMA((1,))],
    )
    return pl.pallas_call(
        functools.partial(_moe_kernel, nrows=nrows, nf=nf),
        grid_spec=grid_spec,
        out_shape=jax.ShapeDtypeStruct((ne, nrows, d), F32),
        compiler_params=_cparams(("arbitrary", "arbitrary")),
        name="moe_experts",
    )(idx, a, gate, w_gate, w_up, w_down)


def _combine_kernel(h_ref, y_ref, m_ref, mn_ref, g_ref, h_out, a_out):
    hn = h_ref[...] + m_ref[0][5:6] * y_ref[...]
    h_out[...] = hn
    mn = mn_ref[0]
    a_out[...] = (_rms(hn, g_ref[...]) * (1.0 + mn[1:2]) + mn[0:1]).astype(a_out.dtype)


def _combine(h, y, mods, mods_next, g_next, tiles_per_sample):
    r, d = h.shape
    tm = ROW_TILE
    row = lambda i: (i, 0)
    return pl.pallas_call(
        _combine_kernel,
        grid=(r // tm,),
        in_specs=[pl.BlockSpec((tm, d), row), pl.BlockSpec((tm, d), row),
                  pl.BlockSpec((1, 6, d), _mod_row_map(tiles_per_sample)),
                  pl.BlockSpec((1, 6, d), _mod_row_map(tiles_per_sample)),
                  pl.BlockSpec((1, d), lambda i: (0, 0))],
        out_specs=[pl.BlockSpec((tm, d), row), pl.BlockSpec((tm, d), row)],
        out_shape=[jax.ShapeDtypeStruct((r, d), F32), jax.ShapeDtypeStruct((r, d), BF16)],
        compiler_params=_cparams(("arbitrary",)),
        name="moe_combine",
    )(h, y, mods, mods_next, g_next.reshape(1, d))


def _final_kernel(h_ref, y_ref, m_ref, g_ref, o_ref):
    hn = h_ref[0] + m_ref[0][5:6] * y_ref[0]
    o_ref[0] = _rms(hn, g_ref[...])


def _final(h3, y3, mods, g, tc):
    b, l, d = h3.shape
    tm = ROW_TILE
    off = tc // tm
    src = lambda bi, i: (bi, i + off, 0)
    return pl.pallas_call(
        _final_kernel,
        grid=(b, (l - tc) // tm),
        in_specs=[pl.BlockSpec((1, tm, d), src), pl.BlockSpec((1, tm, d), src),
                  pl.BlockSpec((1, 6, d), lambda bi, i: (bi, 0, 0)),
                  pl.BlockSpec((1, d), lambda bi, i: (0, 0))],
        out_specs=pl.BlockSpec((1, tm, d), lambda bi, i: (bi, i, 0)),
        out_shape=jax.ShapeDtypeStruct((b, l - tc, d), F32),
        compiler_params=_cparams(("arbitrary", "arbitrary")),
        name="final_norm",
    )(h3, y3, mods, g.reshape(1, d))


def _route(aff, b, l, tc, with_ctx):
    aff3 = aff.reshape(b, l, N_EXPERTS)
    ids, gates = [], []
    segs = [(tc, l - tc)] + ([(0, tc)] if with_ctx else [])
    for start, n in segs:
        cap = EC_CAPACITY * n // N_EXPERTS
        g, idx = lax.top_k(jnp.swapaxes(aff3[:, start:start + n], 1, 2), cap)
        rows = idx + start + (jnp.arange(b, dtype=jnp.int32) * l)[:, None, None]
        ids.append(jnp.swapaxes(rows, 0, 1).reshape(N_EXPERTS, b * cap))
        gates.append(jnp.swapaxes(g, 0, 1).reshape(N_EXPERTS, b * cap))
    idx_all = jnp.concatenate(ids, 1).astype(jnp.int32)
    gate_all = jnp.concatenate(gates, 1)[..., None]
    return idx_all, gate_all


def kernel(x, c, ctx, c_ctx, w_mod, b_mod, norm1_g, norm2_g, w_in, gdn_conv_w, gdn_a_log, gdn_dt_bias,
           gdn_norm_g, sc_conv_w, diff_lambda, diff_norm_g, w_out, w_router, w_e_gate, w_e_up, w_e_down,
           final_norm_g):
    b, t, d = x.shape
    tc = ctx.shape[1]
    l = tc + t
    r = b * l
    depth = w_mod.shape[0]
    tiles = l // ROW_TILE
    nh = GDN_HEADS

    cond = jnp.concatenate([c, c_ctx[None, :], jnp.zeros((8 - b - 1, d), F32)], 0)
    mods_all = _modulation(cond, w_mod, b_mod)[:, :3].reshape(depth, 3, 6, d)
    cos_t, sin_t = _rope_tables(t, tc)

    h = jnp.concatenate([ctx, x], 1).reshape(r, d)
    a = _norm_mod(h, norm1_g[0], mods_all[0], tiles)
    out = None
    for li in range(depth):
        with_ctx = li < depth - 1
        lam_init = 0.8 - 0.6 * math.exp(-0.3 * li)
        wl = w_in[li]
        a_in = 4 * GDN_W
        w_main = jnp.concatenate([wl[:, :a_in], wl[:, a_in + 4 * nh:]], 1).astype(BF16)
        w_ba = jnp.concatenate([wl[:, a_in:a_in + 4 * nh], jnp.zeros((d, LANE - 4 * nh), F32)], 1).astype(BF16)
        u = _matmul(a, w_main, BF16, r // 8, GDN_W)
        ba = _matmul(a, w_ba, F32, r // 8, LANE)
        u3 = u.reshape(b, l, N_MAIN)

        qkv = _gdn_prep(u3, gdn_conv_w[li], tc)
        prm = jnp.zeros((8, LANE), F32)
        prm = prm.at[0, 2 * nh:4 * nh].set(-jnp.exp(gdn_a_log[li].reshape(-1)))
        prm = prm.at[1, 2 * nh:4 * nh].set(gdn_dt_bias[li].reshape(-1))
        o_f, o_b = _gdn(qkv, ba.reshape(b, l, LANE), prm, tc, GDN_CHUNK)
        y_b = _short_conv(u3, sc_conv_w[li], tc)
        qk = _rope_prep(u3, cos_t, sin_t)
        y_c = _diff_attn(qk, u3, diff_lambda[li], diff_norm_g[li], lam_init, tc)

        h, a2, aff = _outproj(o_f.reshape(r, GDN_W), o_b.reshape(r, GDN_W), u, y_b.reshape(r, SC_W),
                              y_c.reshape(r, DIFF_W), w_out[li].astype(BF16), h, mods_all[li],
                              gdn_norm_g[li], norm2_g[li], w_router[li], tiles)
        idx, gate = _route(aff, b, l, tc, with_ctx)
        y = _moe(idx, a2, gate, w_e_gate[li], w_e_up[li], w_e_down[li])
        moe = jnp.zeros((r, d), F32).at[idx.reshape(-1)].add(y.reshape(-1, d))
        if with_ctx:
            h, a = _combine(h, moe, mods_all[li], mods_all[li + 1], norm1_g[li + 1], tiles)
        else:
            out = _final(h.reshape(b, l, d), moe.reshape(b, l, d), mods_all[li], final_norm_g, tc)
    return out
```

```python
import functools
import math

import jax
import jax.numpy as jnp
from jax import lax
from jax.experimental import pallas as pl
from jax.experimental.pallas import tpu as pltpu

F32 = jnp.float32
BF16 = jnp.bfloat16

EPS = 1e-6
GRID_W = 64
GDN_HEADS = 6
GDN_D = 128
GDN_W = GDN_HEADS * GDN_D
SC_W = 512
DIFF_HEADS = 6
DIFF_DH = 64
DIFF_DV = 128
DIFF_W = DIFF_HEADS * DIFF_DV
ROPE_BASE = 10000.0
ROPE_PAIRS = DIFF_DH // 4
N_EXPERTS = 16
EC_CAPACITY = 2

COL_Z = 3 * GDN_W
COL_SC = COL_Z + GDN_W
COL_ATT = COL_SC + 3 * SC_W
N_MAIN = COL_ATT + 3 * DIFF_W
LANE = 128
ROW_TILE = 256
GDN_CHUNK = 128
VMEM_LIMIT = 56 * 1024 * 1024


def _cparams(sem, vmem=VMEM_LIMIT):
    return pltpu.CompilerParams(dimension_semantics=sem, vmem_limit_bytes=vmem)


def _sigmoid(x):
    return 1.0 / (1.0 + jnp.exp(-x))


def _silu(x):
    return x * _sigmoid(x)


def _softplus(x):
    return jnp.maximum(x, 0.0) + jnp.log(1.0 + jnp.exp(-jnp.abs(x)))


def _rms(x, g):
    return x * lax.rsqrt(jnp.mean(x * x, axis=-1, keepdims=True) + EPS) * g


def _dot(a, b, **kw):
    return jnp.dot(a, b, preferred_element_type=F32, **kw)


def _dot_nt(a, b):
    return lax.dot_general(a, b, (((1,), (1,)), ((), ())), preferred_element_type=F32)


def _mod_kernel(s_ref, w_ref, b_ref, o_ref):
    s = _silu(s_ref[...]).astype(BF16)
    o_ref[0] = _dot(s, w_ref[0].astype(BF16)) + b_ref[0]


def _modulation(cond, w_mod, b_mod):
    depth, d, n = w_mod.shape
    tn = 1536
    return pl.pallas_call(
        _mod_kernel,
        grid=(depth, n // tn),
        in_specs=[pl.BlockSpec((8, d), lambda l, j: (0, 0)),
                  pl.BlockSpec((1, d, tn), lambda l, j: (l, 0, j)),
                  pl.BlockSpec((1, 1, tn), lambda l, j: (l, 0, j))],
        out_specs=pl.BlockSpec((1, 8, tn), lambda l, j: (l, 0, j)),
        out_shape=jax.ShapeDtypeStruct((depth, 8, n), F32),
        compiler_params=_cparams(("arbitrary", "arbitrary")),
        name="modulation",
    )(cond, w_mod, b_mod.reshape(depth, 1, n))


def _mod_row_map(tiles_per_sample):
    def index_map(i):
        return (jnp.where(i % tiles_per_sample == 0, 2, i // tiles_per_sample), 0, 0)
    return index_map


def _norm_mod_kernel(h_ref, g_ref, m_ref, a_ref):
    m = m_ref[0]
    a = _rms(h_ref[...], g_ref[...]) * (1.0 + m[1:2]) + m[0:1]
    a_ref[...] = a.astype(a_ref.dtype)


def _norm_mod(h, g, mods, tiles_per_sample):
    r, d = h.shape
    return pl.pallas_call(
        _norm_mod_kernel,
        grid=(r // ROW_TILE,),
        in_specs=[pl.BlockSpec((ROW_TILE, d), lambda i: (i, 0)),
                  pl.BlockSpec((1, d), lambda i: (0, 0)),
                  pl.BlockSpec((1, 6, d), _mod_row_map(tiles_per_sample))],
        out_specs=pl.BlockSpec((ROW_TILE, d), lambda i: (i, 0)),
        out_shape=jax.ShapeDtypeStruct((r, d), BF16),
        compiler_params=_cparams(("arbitrary",)),
        name="norm_mod",
    )(h, g.reshape(1, d), mods)


def _mm_kernel(a_ref, w_ref, o_ref):
    o_ref[...] = _dot(a_ref[...], w_ref[...]).astype(o_ref.dtype)


def _matmul(a, w, out_dtype, tm, tn):
    r, k = a.shape
    n = w.shape[1]
    return pl.pallas_call(
        _mm_kernel,
        grid=(n // tn, r // tm),
        in_specs=[pl.BlockSpec((tm, k), lambda j, i: (i, 0)),
                  pl.BlockSpec((k, tn), lambda j, i: (0, j))],
        out_specs=pl.BlockSpec((tm, tn), lambda j, i: (i, j)),
        out_shape=jax.ShapeDtypeStruct((r, n), out_dtype),
        compiler_params=_cparams(("arbitrary", "arbitrary")),
        name="in_proj",
    )(a, w)


def _conv3(x, w, tc):
    l = x.shape[0]
    row = lax.broadcasted_iota(jnp.int32, x.shape, 0)
    prev = jnp.where((row == 0) | (row == tc), 0.0, pltpu.roll(x, 1, 0))
    nxt = jnp.where((row == tc - 1) | (row == l - 1), 0.0, pltpu.roll(x, l - 1, 0))
    return prev * w[0:1] + x * w[1:2] + nxt * w[2:3]


def _gdn_prep_kernel(u_ref, w_ref, o_ref, *, tc):
    j = pl.program_id(1)
    y = _silu(_conv3(u_ref[0].astype(F32), w_ref[...], tc))
    inv = lax.rsqrt(jnp.sum(y * y, axis=-1, keepdims=True) + EPS)
    nh = GDN_HEADS
    fac = jnp.where(j < nh, inv * GDN_D ** -0.5, jnp.where(j < 2 * nh, inv, 1.0))
    o_ref[0] = (y * fac).astype(o_ref.dtype)


def _gdn_prep(u3, conv_w, tc):
    b, l, _ = u3.shape
    nblk = 3 * GDN_W // LANE
    return pl.pallas_call(
        functools.partial(_gdn_prep_kernel, tc=tc),
        grid=(b, nblk),
        in_specs=[pl.BlockSpec((1, l, LANE), lambda bi, j: (bi, 0, j)),
                  pl.BlockSpec((3, LANE), lambda bi, j: (0, j))],
        out_specs=pl.BlockSpec((1, l, LANE), lambda bi, j: (bi, 0, j)),
        out_shape=jax.ShapeDtypeStruct((b, l, 3 * GDN_W), BF16),
        compiler_params=_cparams(("arbitrary", "arbitrary")),
        name="gdn_prep",
    )(u3, conv_w)


def _sconv_kernel(b_ref, c_ref, x_ref, w_ref, o_ref, *, tc):
    inner = c_ref[0].astype(F32) * x_ref[0].astype(F32)
    o_ref[0] = (b_ref[0].astype(F32) * _conv3(inner, w_ref[...], tc)).astype(o_ref.dtype)


def _short_conv(u3, conv_w, tc):
    b, l, _ = u3.shape
    nblk = SC_W // LANE
    base = COL_SC // LANE

    def spec(k):
        return pl.BlockSpec((1, l, LANE), lambda bi, j: (bi, 0, base + k * nblk + j))

    return pl.pallas_call(
        functools.partial(_sconv_kernel, tc=tc),
        grid=(b, nblk),
        in_specs=[spec(0), spec(1), spec(2), pl.BlockSpec((3, LANE), lambda bi, j: (0, j))],
        out_specs=pl.BlockSpec((1, l, LANE), lambda bi, j: (bi, 0, j)),
        out_shape=jax.ShapeDtypeStruct((b, l, SC_W), BF16),
        compiler_params=_cparams(("arbitrary", "arbitrary")),
        name="short_conv",
    )(u3, u3, u3, conv_w)


def _rope(x, cos, sin):
    lane = lax.broadcasted_iota(jnp.int32, x.shape, 1)
    swapped = jnp.where(lane % 32 < 16, pltpu.roll(x, LANE - 16, 1), pltpu.roll(x, 16, 1))
    return x * cos + swapped * sin


def _attn_prep_kernel(q_ref, k_ref, v_ref, cos_ref, sin_ref, qt_ref, ko_ref, vt_ref):
    cos = cos_ref[...]
    sin = sin_ref[...]
    q = _rope(q_ref[0].astype(F32), cos, sin) * DIFF_DH ** -0.5
    qt_ref[0, 0] = q.T.astype(qt_ref.dtype)
    ko_ref[0] = _rope(k_ref[0].astype(F32), cos, sin).astype(ko_ref.dtype)
    vt_ref[0, 0] = v_ref[0].astype(F32).T.astype(vt_ref.dtype)


def _attn_prep(u3, cos_t, sin_t):
    b, l, _ = u3.shape
    nh = DIFF_HEADS
    base = COL_ATT // LANE
    tbl = pl.BlockSpec((l, LANE), lambda bi, j: (0, 0))
    tspec = pl.BlockSpec((1, 1, LANE, l), lambda bi, j: (bi, j, 0, 0))
    tshape = jax.ShapeDtypeStruct((b, nh, LANE, l), BF16)
    return pl.pallas_call(
        _attn_prep_kernel,
        grid=(b, nh),
        in_specs=[pl.BlockSpec((1, l, LANE), lambda bi, j: (bi, 0, base + j)),
                  pl.BlockSpec((1, l, LANE), lambda bi, j: (bi, 0, base + nh + j)),
                  pl.BlockSpec((1, l, LANE), lambda bi, j: (bi, 0, base + 2 * nh + j)),
                  tbl, tbl],
        out_specs=[tspec, pl.BlockSpec((1, l, LANE), lambda bi, j: (bi, 0, j)), tspec],
        out_shape=[tshape, jax.ShapeDtypeStruct((b, l, DIFF_W), BF16), tshape],
        compiler_params=_cparams(("arbitrary", "arbitrary")),
        name="attn_prep",
    )(u3, u3, u3, cos_t, sin_t)


def _rope_tables(t, tc):
    rows = t // GRID_W
    row = jnp.repeat(jnp.arange(rows, dtype=F32), GRID_W)
    col = jnp.tile(jnp.arange(GRID_W, dtype=F32), rows)
    inv = ROPE_BASE ** (-jnp.arange(ROPE_PAIRS, dtype=F32) / ROPE_PAIRS)
    lane = jnp.arange(LANE)
    axis = (lane // 32) % 2
    pos = jnp.where(axis[None, :] == 0, row[:, None], col[:, None])
    ang = pos * inv[lane % 16][None, :]
    sign = jnp.where(lane % 32 < 16, -1.0, 1.0)[None, :]
    cos_t = jnp.concatenate([jnp.ones((tc, LANE), F32), jnp.cos(ang)], 0)
    sin_t = jnp.concatenate([jnp.zeros((tc, LANE), F32), jnp.sin(ang) * sign], 0)
    return cos_t, sin_t


def _gdn_kernel(qf_ref, kf_ref, vf_ref, gf_ref, qb_ref, kb_ref, vb_ref, gb_ref, prm_ref,
                of_ref, ob_ref, s_ref, *, chunk):
    c = chunk
    nh = GDN_HEADS

    @pl.when(pl.program_id(1) == 0)
    def _():
        s_ref[...] = jnp.zeros_like(s_ref)

    ii = lax.broadcasted_iota(jnp.int32, (c, c), 0)
    jj = lax.broadcasted_iota(jnp.int32, (c, c), 1)
    neg_a = prm_ref[0:1, :]
    dt_b = prm_ref[1:2, :]
    blk = [(ii >> sh) == (jj >> sh) for sh in range(3, int(math.log2(c)) + 1)]
    dirs = ((qf_ref, kf_ref, vf_ref, gf_ref, of_ref), (qb_ref, kb_ref, vb_ref, gb_ref, ob_ref))
    offm = [blk[lvl] & ~blk[lvl - 1] for lvl in range(1, len(blk))]

    units = []
    for d, (q_ref, k_ref, v_ref, g_ref, o_ref) in enumerate(dirs):
        incl = (ii >= jj) if d == 0 else (ii <= jj)
        strict = (ii > jj) if d == 0 else (ii < jj)
        raw = g_ref[0]
        g_all = neg_a * _softplus(raw + dt_b)
        beta_all = _sigmoid(raw)
        tri = incl.astype(F32)
        cum = _dot(tri, g_all, precision=lax.Precision.HIGHEST)
        cum_t = _dot(g_all.T, tri.T, precision=lax.Precision.HIGHEST)
        last = c - 1 if d == 0 else 0
        for h in range(nh):
            cb = d * nh + h
            cg = 2 * nh + cb
            sl = slice(h * GDN_D, (h + 1) * GDN_D)
            cc = cum[:, cg:cg + 1]
            units.append(dict(
                cb=cb, sl=sl, o_ref=o_ref, incl=incl, strict=strict, cc=cc,
                beta=beta_all[:, cb:cb + 1], cr=cum_t[cg:cg + 1, :], tot=cum[last:last + 1, cg:cg + 1],
                q=q_ref[0, :, sl], k=k_ref[0, :, sl], v=v_ref[0, :, sl]))

    for un in units:
        un["kbeta"] = un["k"].astype(F32) * un["beta"]
        un["gram"] = _dot_nt(jnp.concatenate([un["kbeta"].astype(BF16), un["q"]], axis=0), un["k"])
    for un in units:
        decay = jnp.exp(jnp.where(un["incl"], un["cc"] - un["cr"], -1e30))
        un["low"] = jnp.where(un["strict"], un["gram"][:c] * decay, 0.0)
        un["attn"] = (un["gram"][c:] * decay).astype(BF16)
        un["p"] = jnp.where(blk[0], -un["low"], 0.0)
        un["n"] = un["p"]
    for _ in range(2):
        for un in units:
            pb = un["p"].astype(BF16)
            un["p"] = _dot(pb, pb)
        for un in units:
            un["n"] = un["n"] + un["p"] + _dot(un["n"].astype(BF16), un["p"].astype(BF16))
    for om in offm:
        for un in units:
            off = jnp.where(om, un["low"], 0.0).astype(BF16)
            un["m1"] = off + _dot(off, un["n"].astype(BF16))
        for un in units:
            un["n"] = un["n"] - un["m1"] - _dot(un["n"].astype(BF16), un["m1"].astype(BF16))
    for un in units:
        eg = jnp.exp(un["cc"])
        rhs = jnp.concatenate([un["v"].astype(F32) * un["beta"], un["kbeta"] * eg], axis=1)
        sol = rhs + _dot(un["n"].astype(BF16), rhs.astype(BF16))
        un["u"] = sol[:, :GDN_D]
        qd = un["q"].astype(F32) * eg
        un["wq"] = jnp.concatenate([sol[:, GDN_D:].astype(BF16), qd.astype(BF16)], axis=0)
        un["kdt"] = (un["k"].astype(F32) * jnp.exp(un["tot"] - un["cc"])).T.astype(BF16)
    for un in units:
        un["s"] = s_ref[un["cb"]]
        un["ws"] = _dot(un["wq"], un["s"].astype(BF16))
    for un in units:
        un["v_new"] = (un["u"] - un["ws"][:c]).astype(BF16)
        un["o_ref"][0, :, un["sl"]] = un["ws"][c:] + _dot(un["attn"], un["v_new"])
    for un in units:
        s_ref[un["cb"]] = un["s"] * jnp.exp(un["tot"]) + _dot(un["kdt"], un["v_new"])


def _gdn(qkv, ba, prm, tc, chunk):
    b, l, _ = qkv.shape
    nc = l // chunk
    ncc = tc // chunk

    def fwd(col):
        return lambda bi, n: (bi, n, col)

    def bwd(col):
        return lambda bi, n: (bi, jnp.where(n < ncc, ncc - 1 - n, nc - 1 + ncc - n), col)

    blk = (1, chunk, GDN_W)
    gblk = (1, chunk, LANE)
    out = jax.ShapeDtypeStruct((b, l, GDN_W), F32)
    return pl.pallas_call(
        functools.partial(_gdn_kernel, chunk=chunk),
        grid=(b, nc),
        in_specs=[pl.BlockSpec(blk, fwd(0)), pl.BlockSpec(blk, fwd(1)), pl.BlockSpec(blk, fwd(2)),
                  pl.BlockSpec(gblk, fwd(0)),
                  pl.BlockSpec(blk, bwd(0)), pl.BlockSpec(blk, bwd(1)), pl.BlockSpec(blk, bwd(2)),
                  pl.BlockSpec(gblk, bwd(0)),
                  pl.BlockSpec((8, LANE), lambda bi, n: (0, 0))],
        out_specs=[pl.BlockSpec(blk, fwd(0)), pl.BlockSpec(blk, bwd(0))],
        out_shape=[out, out],
        scratch_shapes=[pltpu.VMEM((2 * GDN_HEADS, GDN_D, GDN_D), F32)],
        compiler_params=_cparams(("arbitrary", "arbitrary")),
        name="gdn",
    )(qkv, qkv, qkv, ba, qkv, qkv, qkv, ba, prm)


def _attn_kernel(qt_ref, k_ref, vt_ref, lp_ref, g_ref, o_ref, acc_sc, *, tq, tk, nk_ctx, nk_all, lam_init):
    qt = qt_ref[0, 0]
    sub = lax.broadcasted_iota(jnp.int32, qt.shape, 0)
    zero = jnp.zeros_like(qt)
    qq = jnp.concatenate([jnp.where(sub < DIFF_DH, qt, zero), jnp.where(sub >= DIFF_DH, qt, zero)], axis=1)
    lp = lp_ref[...]
    lam = (jnp.exp(jnp.sum(lp[0:1] * lp[1:2], axis=1, keepdims=True))
           - jnp.exp(jnp.sum(lp[2:3] * lp[3:4], axis=1, keepdims=True)) + lam_init)

    def run(nkv):
        m = jnp.full((1, 2 * tq), -jnp.inf, F32)
        l = jnp.zeros((1, 2 * tq), F32)
        for c in range(nkv):
            kc = k_ref[0, c * tk:(c + 1) * tk, :]
            vc = vt_ref[0, 0, :, c * tk:(c + 1) * tk]
            s = _dot(kc, qq)
            m_new = jnp.maximum(m, jnp.max(s, axis=0, keepdims=True))
            p = jnp.exp(s - m_new)
            alpha = jnp.exp(m - m_new)
            l = alpha * l + jnp.sum(p, axis=0, keepdims=True)
            pv = _dot(vc, p.astype(BF16))
            acc_sc[...] = pv if c == 0 else alpha * acc_sc[...] + pv
            m = m_new
        on = acc_sc[...] / l
        o = on[:, :tq] - lam * on[:, tq:]
        y = o * lax.rsqrt(jnp.mean(o * o, axis=0, keepdims=True) + EPS) * g_ref[...] * (1.0 - lam_init)
        o_ref[0] = y.T.astype(o_ref.dtype)

    is_ctx = pl.program_id(2) == 0

    @pl.when(is_ctx)
    def _():
        run(nk_ctx)

    @pl.when(jnp.logical_not(is_ctx))
    def _():
        run(nk_all)


def _diff_attn(qt, k, vt, lam_params, norm_g, lam_init, tc):
    b, l, _ = k.shape
    tq = ROW_TILE
    tk = 256
    kern = functools.partial(_attn_kernel, tq=tq, tk=tk, nk_ctx=tc // tk, nk_all=l // tk, lam_init=lam_init)
    return pl.pallas_call(
        kern,
        grid=(b, DIFF_HEADS, l // tq),
        in_specs=[pl.BlockSpec((1, 1, LANE, tq), lambda bi, h, i: (bi, h, 0, i)),
                  pl.BlockSpec((1, l, LANE), lambda bi, h, i: (bi, 0, h)),
                  pl.BlockSpec((1, 1, LANE, l), lambda bi, h, i: (bi, h, 0, 0)),
                  pl.BlockSpec((4, DIFF_DH), lambda bi, h, i: (0, 0)),
                  pl.BlockSpec((DIFF_DV, 1), lambda bi, h, i: (0, 0))],
        out_specs=pl.BlockSpec((1, tq, LANE), lambda bi, h, i: (bi, i, h)),
        out_shape=jax.ShapeDtypeStruct((b, l, DIFF_W), BF16),
        scratch_shapes=[pltpu.VMEM((DIFF_DV, 2 * tq), F32)],
        compiler_params=_cparams(("arbitrary", "arbitrary", "arbitrary")),
        name="diff_attn",
    )(qt, k, vt, lam_params, norm_g.reshape(DIFF_DV, 1))


def _outproj_kernel(of_ref, ob_ref, z_ref, yb_ref, yc_ref, w_ref, h_ref, m_ref, gn_ref, g2_ref, wr_ref,
                    h_out, a_out, aff_out):
    o = of_ref[...] + ob_ref[...]
    gn = gn_ref[...]
    parts = []
    for hd in range(GDN_HEADS):
        sl = slice(hd * GDN_D, (hd + 1) * GDN_D)
        parts.append((_rms(o[:, sl], gn) * _silu(z_ref[:, sl].astype(F32))).astype(BF16))
    ya = jnp.concatenate(parts, axis=1)
    acc = _dot(ya, w_ref[0:GDN_W, :])
    acc += _dot(yb_ref[...], w_ref[GDN_W:GDN_W + SC_W, :])
    acc += _dot(yc_ref[...], w_ref[GDN_W + SC_W:, :])
    m = m_ref[0]
    hn = h_ref[...] + m[2:3] * acc
    h_out[...] = hn
    a = _rms(hn, g2_ref[...]) * (1.0 + m[4:5]) + m[3:4]
    a_out[...] = a
    logits = _dot(a, wr_ref[...], precision=lax.Precision.HIGHEST)
    e = jnp.exp(logits - jnp.max(logits, axis=-1, keepdims=True))
    aff_out[...] = e / jnp.sum(e, axis=-1, keepdims=True)


def _outproj(o_f, o_b, u, y_b, y_c, w_out, h, mods, gdn_g, g2, w_router, tiles_per_sample):
    r, d = h.shape
    tm = ROW_TILE
    zb = COL_Z // GDN_W
    row = lambda i: (i, 0)
    const = lambda i: (0, 0)
    return pl.pallas_call(
        _outproj_kernel,
        grid=(r // tm,),
        in_specs=[pl.BlockSpec((tm, GDN_W), row), pl.BlockSpec((tm, GDN_W), row),
                  pl.BlockSpec((tm, GDN_W), lambda i: (i, zb)),
                  pl.BlockSpec((tm, SC_W), row), pl.BlockSpec((tm, DIFF_W), row),
                  pl.BlockSpec((d, d), const), pl.BlockSpec((tm, d), row),
                  pl.BlockSpec((1, 6, d), _mod_row_map(tiles_per_sample)),
                  pl.BlockSpec((1, GDN_D), const), pl.BlockSpec((1, d), const),
                  pl.BlockSpec((d, N_EXPERTS), const)],
        out_specs=[pl.BlockSpec((tm, d), row), pl.BlockSpec((tm, d), row),
                   pl.BlockSpec((tm, N_EXPERTS), row)],
        out_shape=[jax.ShapeDtypeStruct((r, d), F32), jax.ShapeDtypeStruct((r, d), F32),
                   jax.ShapeDtypeStruct((r, N_EXPERTS), F32)],
        compiler_params=_cparams(("arbitrary",)),
        name="out_proj",
    )(o_f, o_b, u, y_b, y_c, w_out, h, mods, gdn_g.reshape(1, GDN_D), g2.reshape(1, d), w_router)


def _moe_kernel(idx_ref, a_hbm, gate_ref, wg_ref, wu_ref, wd_ref, y_ref, x32, xb, sem, *, nrows, nf):
    e = pl.program_id(0)
    f = pl.program_id(1)

    def gather(expert):
        def issue(r, carry):
            pltpu.make_async_copy(a_hbm.at[pl.ds(idx_ref[expert, r], 1), :], x32.at[pl.ds(r, 1), :],
                                  sem.at[0]).start()
            return carry
        lax.fori_loop(0, nrows, issue, 0, unroll=8)

    @pl.when((e == 0) & (f == 0))
    def _():
        gather(0)

    @pl.when(f == 0)
    def _():
        pltpu.make_async_copy(a_hbm.at[pl.ds(0, nrows), :], x32, sem.at[0]).wait()
        xb[...] = x32[...].astype(BF16)

    @pl.when((f == 0) & (e + 1 < pl.num_programs(0)))
    def _():
        gather(e + 1)

    x = xb[...]
    hid = _silu(_dot(x, wg_ref[0, 0].astype(BF16))) * _dot(x, wu_ref[0, 0].astype(BF16))
    part = _dot(hid.astype(BF16), wd_ref[0, 0].astype(BF16))

    @pl.when(f == 0)
    def _():
        y_ref[0] = part

    @pl.when(f > 0)
    def _():
        y_ref[0] += part

    @pl.when(f == nf - 1)
    def _():
        y_ref[0] = y_ref[0] * gate_ref[0]


def _moe(idx, a, gate, w_gate, w_up, w_down, li):
    ne, nrows = idx.shape
    d = a.shape[1]
    ff = w_gate.shape[3]
    tf = 256
    nf = ff // tf
    grid_spec = pltpu.PrefetchScalarGridSpec(
        num_scalar_prefetch=1,
        grid=(ne, nf),
        in_specs=[pl.BlockSpec(memory_space=pl.ANY),
                  pl.BlockSpec((1, nrows, 1), lambda e, f, idx_ref: (e, 0, 0)),
                  pl.BlockSpec((1, 1, d, tf), lambda e, f, idx_ref: (li, e, 0, f)),
                  pl.BlockSpec((1, 1, d, tf), lambda e, f, idx_ref: (li, e, 0, f)),
                  pl.BlockSpec((1, 1, tf, d), lambda e, f, idx_ref: (li, e, f, 0))],
        out_specs=pl.BlockSpec((1, nrows, d), lambda e, f, idx_ref: (e, 0, 0)),
        scratch_shapes=[pltpu.VMEM((nrows, d), F32), pltpu.VMEM((nrows, d), BF16),
                        pltpu.SemaphoreType.DMA((1,))],
    )
    return pl.pallas_call(
        functools.partial(_moe_kernel, nrows=nrows, nf=nf),
        grid_spec=grid_spec,
        out_shape=jax.ShapeDtypeStruct((ne, nrows, d), F32),
        compiler_params=_cparams(("arbitrary", "arbitrary")),
        name="moe_experts",
    )(idx, a, gate, w_gate, w_up, w_down)


def _combine_kernel(h_ref, y_ref, m_ref, mn_ref, g_ref, h_out, a_out):
    hn = h_ref[...] + m_ref[0][5:6] * y_ref[...]
    h_out[...] = hn
    mn = mn_ref[0]
    a_out[...] = (_rms(hn, g_ref[...]) * (1.0 + mn[1:2]) + mn[0:1]).astype(a_out.dtype)


def _combine(h, y, mods, mods_next, g_next, tiles_per_sample):
    r, d = h.shape
    tm = ROW_TILE
    row = lambda i: (i, 0)
    return pl.pallas_call(
        _combine_kernel,
        grid=(r // tm,),
        in_specs=[pl.BlockSpec((tm, d), row), pl.BlockSpec((tm, d), row),
                  pl.BlockSpec((1, 6, d), _mod_row_map(tiles_per_sample)),
                  pl.BlockSpec((1, 6, d), _mod_row_map(tiles_per_sample)),
                  pl.BlockSpec((1, d), lambda i: (0, 0))],
        out_specs=[pl.BlockSpec((tm, d), row), pl.BlockSpec((tm, d), row)],
        out_shape=[jax.ShapeDtypeStruct((r, d), F32), jax.ShapeDtypeStruct((r, d), BF16)],
        compiler_params=_cparams(("arbitrary",)),
        name="moe_combine",
    )(h, y, mods, mods_next, g_next.reshape(1, d))


def _final_kernel(h_ref, y_ref, m_ref, g_ref, o_ref):
    hn = h_ref[0] + m_ref[0][5:6] * y_ref[0]
    o_ref[0] = _rms(hn, g_ref[...])


def _final(h3, y3, mods, g, tc):
    b, l, d = h3.shape
    tm = ROW_TILE
    off = tc // tm
    src = lambda bi, i: (bi, i + off, 0)
    return pl.pallas_call(
        _final_kernel,
        grid=(b, (l - tc) // tm),
        in_specs=[pl.BlockSpec((1, tm, d), src), pl.BlockSpec((1, tm, d), src),
                  pl.BlockSpec((1, 6, d), lambda bi, i: (bi, 0, 0)),
                  pl.BlockSpec((1, d), lambda bi, i: (0, 0))],
        out_specs=pl.BlockSpec((1, tm, d), lambda bi, i: (bi, i, 0)),
        out_shape=jax.ShapeDtypeStruct((b, l - tc, d), F32),
        compiler_params=_cparams(("arbitrary", "arbitrary")),
        name="final_norm",
    )(h3, y3, mods, g.reshape(1, d))


def _route(aff, b, l, tc, with_ctx):
    aff3 = aff.reshape(b, l, N_EXPERTS)
    ids, gates = [], []
    segs = [(tc, l - tc)] + ([(0, tc)] if with_ctx else [])
    for start, n in segs:
        cap = EC_CAPACITY * n // N_EXPERTS
        g, idx = lax.top_k(jnp.swapaxes(aff3[:, start:start + n], 1, 2), cap)
        rows = idx + start + (jnp.arange(b, dtype=jnp.int32) * l)[:, None, None]
        ids.append(jnp.swapaxes(rows, 0, 1).reshape(N_EXPERTS, b * cap))
        gates.append(jnp.swapaxes(g, 0, 1).reshape(N_EXPERTS, b * cap))
    idx_all = jnp.concatenate(ids, 1).astype(jnp.int32)
    gate_all = jnp.concatenate(gates, 1)[..., None]
    return idx_all, gate_all


def kernel(x, c, ctx, c_ctx, w_mod, b_mod, norm1_g, norm2_g, w_in, gdn_conv_w, gdn_a_log, gdn_dt_bias,
           gdn_norm_g, sc_conv_w, diff_lambda, diff_norm_g, w_out, w_router, w_e_gate, w_e_up, w_e_down,
           final_norm_g):
    b, t, d = x.shape
    tc = ctx.shape[1]
    l = tc + t
    r = b * l
    depth = w_mod.shape[0]
    tiles = l // ROW_TILE
    nh = GDN_HEADS

    cond = jnp.concatenate([c, c_ctx[None, :], jnp.zeros((8 - b - 1, d), F32)], 0)
    mods_all = _modulation(cond, w_mod, b_mod)[:, :3].reshape(depth, 3, 6, d)
    cos_t, sin_t = _rope_tables(t, tc)

    h = jnp.concatenate([ctx, x], 1).reshape(r, d)
    a = _norm_mod(h, norm1_g[0], mods_all[0], tiles)
    out = None
    for li in range(depth):
        with_ctx = li < depth - 1
        lam_init = 0.8 - 0.6 * math.exp(-0.3 * li)
        wl = w_in[li]
        a_in = 4 * GDN_W
        w_main = jnp.concatenate([wl[:, :a_in], wl[:, a_in + 4 * nh:]], 1).astype(BF16)
        w_ba = jnp.concatenate([wl[:, a_in:a_in + 4 * nh], jnp.zeros((d, LANE - 4 * nh), F32)], 1).astype(BF16)
        u = _matmul(a, w_main, BF16, r // 8, GDN_W)
        ba = _matmul(a, w_ba, F32, r // 8, LANE)
        u3 = u.reshape(b, l, N_MAIN)

        qkv = _gdn_prep(u3, gdn_conv_w[li], tc)
        prm = jnp.zeros((8, LANE), F32)
        prm = prm.at[0, 2 * nh:4 * nh].set(-jnp.exp(gdn_a_log[li].reshape(-1)))
        prm = prm.at[1, 2 * nh:4 * nh].set(gdn_dt_bias[li].reshape(-1))
        o_f, o_b = _gdn(qkv, ba.reshape(b, l, LANE), prm, tc, GDN_CHUNK)
        y_b = _short_conv(u3, sc_conv_w[li], tc)
        q_t, k_r, v_t = _attn_prep(u3, cos_t, sin_t)
        y_c = _diff_attn(q_t, k_r, v_t, diff_lambda[li], diff_norm_g[li], lam_init, tc)

        h, a2, aff = _outproj(o_f.reshape(r, GDN_W), o_b.reshape(r, GDN_W), u, y_b.reshape(r, SC_W),
                              y_c.reshape(r, DIFF_W), w_out[li].astype(BF16), h, mods_all[li],
                              gdn_norm_g[li], norm2_g[li], w_router[li], tiles)
        idx, gate = _route(aff, b, l, tc, with_ctx)
        y = _moe(idx, a2, gate, w_e_gate, w_e_up, w_e_down, li)
        moe = jnp.zeros((r, d), F32).at[idx.reshape(-1)].add(y.reshape(-1, d))
        if with_ctx:
            h, a = _combine(h, moe, mods_all[li], mods_all[li + 1], norm1_g[li + 1], tiles)
        else:
            out = _final(h.reshape(b, l, d), moe.reshape(b, l, d), mods_all[li], final_norm_g, tc)
    return out
```

```python
import functools
import math

import jax
import jax.numpy as jnp
from jax import lax
from jax.experimental import pallas as pl
from jax.experimental.pallas import tpu as pltpu

F32 = jnp.float32
BF16 = jnp.bfloat16

EPS = 1e-6
GRID_W = 64
GDN_HEADS = 6
GDN_D = 128
GDN_W = GDN_HEADS * GDN_D
SC_W = 512
DIFF_HEADS = 6
DIFF_DH = 64
DIFF_DV = 128
DIFF_W = DIFF_HEADS * DIFF_DV
ROPE_BASE = 10000.0
ROPE_PAIRS = DIFF_DH // 4
N_EXPERTS = 16
EC_CAPACITY = 2

COL_Z = 3 * GDN_W
COL_SC = COL_Z + GDN_W
COL_ATT = COL_SC + 3 * SC_W
N_MAIN = COL_ATT + 3 * DIFF_W
LANE = 128
ROW_TILE = 256
GDN_CHUNK = 128
VT_ROWS = DIFF_DV + 16
VMEM_LIMIT = 56 * 1024 * 1024


def _cparams(sem, vmem=VMEM_LIMIT):
    return pltpu.CompilerParams(dimension_semantics=sem, vmem_limit_bytes=vmem)


def _sigmoid(x):
    return 1.0 / (1.0 + jnp.exp(-x))


def _silu(x):
    return x * _sigmoid(x)


def _softplus(x):
    return jnp.maximum(x, 0.0) + jnp.log(1.0 + jnp.exp(-jnp.abs(x)))


def _rms(x, g):
    return x * lax.rsqrt(jnp.mean(x * x, axis=-1, keepdims=True) + EPS) * g


def _dot(a, b, **kw):
    return jnp.dot(a, b, preferred_element_type=F32, **kw)


def _dot_nt(a, b):
    return lax.dot_general(a, b, (((1,), (1,)), ((), ())), preferred_element_type=F32)


def _mod_kernel(s_ref, w_ref, b_ref, o_ref):
    s = _silu(s_ref[...]).astype(BF16)
    o_ref[0] = _dot(s, w_ref[0].astype(BF16)) + b_ref[0]


def _modulation(cond, w_mod, b_mod):
    depth, d, n = w_mod.shape
    tn = 1536
    return pl.pallas_call(
        _mod_kernel,
        grid=(depth, n // tn),
        in_specs=[pl.BlockSpec((8, d), lambda l, j: (0, 0)),
                  pl.BlockSpec((1, d, tn), lambda l, j: (l, 0, j)),
                  pl.BlockSpec((1, 1, tn), lambda l, j: (l, 0, j))],
        out_specs=pl.BlockSpec((1, 8, tn), lambda l, j: (l, 0, j)),
        out_shape=jax.ShapeDtypeStruct((depth, 8, n), F32),
        compiler_params=_cparams(("arbitrary", "arbitrary")),
        name="modulation",
    )(cond, w_mod, b_mod.reshape(depth, 1, n))


def _mod_row_map(tiles_per_sample):
    def index_map(i):
        return (jnp.where(i % tiles_per_sample == 0, 2, i // tiles_per_sample), 0, 0)
    return index_map


def _norm_mod_kernel(h_ref, g_ref, m_ref, a_ref):
    m = m_ref[0]
    a = _rms(h_ref[...], g_ref[...]) * (1.0 + m[1:2]) + m[0:1]
    a_ref[...] = a.astype(a_ref.dtype)


def _norm_mod(h, g, mods, tiles_per_sample):
    r, d = h.shape
    return pl.pallas_call(
        _norm_mod_kernel,
        grid=(r // ROW_TILE,),
        in_specs=[pl.BlockSpec((ROW_TILE, d), lambda i: (i, 0)),
                  pl.BlockSpec((1, d), lambda i: (0, 0)),
                  pl.BlockSpec((1, 6, d), _mod_row_map(tiles_per_sample))],
        out_specs=pl.BlockSpec((ROW_TILE, d), lambda i: (i, 0)),
        out_shape=jax.ShapeDtypeStruct((r, d), BF16),
        compiler_params=_cparams(("arbitrary",)),
        name="norm_mod",
    )(h, g.reshape(1, d), mods)


def _mm_kernel(a_ref, w_ref, o_ref):
    o_ref[...] = _dot(a_ref[...], w_ref[...]).astype(o_ref.dtype)


def _matmul(a, w, out_dtype, tm, tn):
    r, k = a.shape
    n = w.shape[1]
    return pl.pallas_call(
        _mm_kernel,
        grid=(n // tn, r // tm),
        in_specs=[pl.BlockSpec((tm, k), lambda j, i: (i, 0)),
                  pl.BlockSpec((k, tn), lambda j, i: (0, j))],
        out_specs=pl.BlockSpec((tm, tn), lambda j, i: (i, j)),
        out_shape=jax.ShapeDtypeStruct((r, n), out_dtype),
        compiler_params=_cparams(("arbitrary", "arbitrary")),
        name="in_proj",
    )(a, w)


def _conv3(x, w, tc):
    l = x.shape[0]
    row = lax.broadcasted_iota(jnp.int32, x.shape, 0)
    prev = jnp.where((row == 0) | (row == tc), 0.0, pltpu.roll(x, 1, 0))
    nxt = jnp.where((row == tc - 1) | (row == l - 1), 0.0, pltpu.roll(x, l - 1, 0))
    return prev * w[0:1] + x * w[1:2] + nxt * w[2:3]


def _gdn_prep_kernel(u_ref, w_ref, o_ref, *, tc):
    j = pl.program_id(1)
    y = _silu(_conv3(u_ref[0].astype(F32), w_ref[...], tc))
    inv = lax.rsqrt(jnp.sum(y * y, axis=-1, keepdims=True) + EPS)
    nh = GDN_HEADS
    fac = jnp.where(j < nh, inv * GDN_D ** -0.5, jnp.where(j < 2 * nh, inv, 1.0))
    o_ref[0] = (y * fac).astype(o_ref.dtype)


def _gdn_prep(u3, conv_w, tc):
    b, l, _ = u3.shape
    nblk = 3 * GDN_W // LANE
    return pl.pallas_call(
        functools.partial(_gdn_prep_kernel, tc=tc),
        grid=(b, nblk),
        in_specs=[pl.BlockSpec((1, l, LANE), lambda bi, j: (bi, 0, j)),
                  pl.BlockSpec((3, LANE), lambda bi, j: (0, j))],
        out_specs=pl.BlockSpec((1, l, LANE), lambda bi, j: (bi, 0, j)),
        out_shape=jax.ShapeDtypeStruct((b, l, 3 * GDN_W), BF16),
        compiler_params=_cparams(("arbitrary", "arbitrary")),
        name="gdn_prep",
    )(u3, conv_w)


def _sconv_kernel(b_ref, c_ref, x_ref, w_ref, o_ref, *, tc):
    inner = c_ref[0].astype(F32) * x_ref[0].astype(F32)
    o_ref[0] = (b_ref[0].astype(F32) * _conv3(inner, w_ref[...], tc)).astype(o_ref.dtype)


def _short_conv(u3, conv_w, tc):
    b, l, _ = u3.shape
    nblk = SC_W // LANE
    base = COL_SC // LANE

    def spec(k):
        return pl.BlockSpec((1, l, LANE), lambda bi, j: (bi, 0, base + k * nblk + j))

    return pl.pallas_call(
        functools.partial(_sconv_kernel, tc=tc),
        grid=(b, nblk),
        in_specs=[spec(0), spec(1), spec(2), pl.BlockSpec((3, LANE), lambda bi, j: (0, j))],
        out_specs=pl.BlockSpec((1, l, LANE), lambda bi, j: (bi, 0, j)),
        out_shape=jax.ShapeDtypeStruct((b, l, SC_W), BF16),
        compiler_params=_cparams(("arbitrary", "arbitrary")),
        name="short_conv",
    )(u3, u3, u3, conv_w)


def _rope(x, cos, sin):
    lane = lax.broadcasted_iota(jnp.int32, x.shape, 1)
    swapped = jnp.where(lane % 32 < 16, pltpu.roll(x, LANE - 16, 1), pltpu.roll(x, 16, 1))
    return x * cos + swapped * sin


def _attn_prep_kernel(q_ref, k_ref, v_ref, cos_ref, sin_ref, qt_ref, ko_ref, vt_ref):
    cos = cos_ref[...]
    sin = sin_ref[...]
    q = _rope(q_ref[0].astype(F32), cos, sin) * DIFF_DH ** -0.5
    qt_ref[0, 0] = q.T.astype(qt_ref.dtype)
    ko_ref[0] = _rope(k_ref[0].astype(F32), cos, sin).astype(ko_ref.dtype)
    vt_ref[0, 0, 0:DIFF_DV, :] = v_ref[0].astype(F32).T.astype(vt_ref.dtype)
    vt_ref[0, 0, DIFF_DV:, :] = jnp.ones((VT_ROWS - DIFF_DV, v_ref.shape[1]), vt_ref.dtype)


def _attn_prep(u3, cos_t, sin_t):
    b, l, _ = u3.shape
    nh = DIFF_HEADS
    base = COL_ATT // LANE
    tbl = pl.BlockSpec((l, LANE), lambda bi, j: (0, 0))
    tspec = pl.BlockSpec((1, 1, LANE, l), lambda bi, j: (bi, j, 0, 0))
    tshape = jax.ShapeDtypeStruct((b, nh, LANE, l), BF16)
    vspec = pl.BlockSpec((1, 1, VT_ROWS, l), lambda bi, j: (bi, j, 0, 0))
    vshape = jax.ShapeDtypeStruct((b, nh, VT_ROWS, l), BF16)
    return pl.pallas_call(
        _attn_prep_kernel,
        grid=(b, nh),
        in_specs=[pl.BlockSpec((1, l, LANE), lambda bi, j: (bi, 0, base + j)),
                  pl.BlockSpec((1, l, LANE), lambda bi, j: (bi, 0, base + nh + j)),
                  pl.BlockSpec((1, l, LANE), lambda bi, j: (bi, 0, base + 2 * nh + j)),
                  tbl, tbl],
        out_specs=[tspec, pl.BlockSpec((1, l, LANE), lambda bi, j: (bi, 0, j)), vspec],
        out_shape=[tshape, jax.ShapeDtypeStruct((b, l, DIFF_W), BF16), vshape],
        compiler_params=_cparams(("arbitrary", "arbitrary")),
        name="attn_prep",
    )(u3, u3, u3, cos_t, sin_t)


def _rope_tables(t, tc):
    rows = t // GRID_W
    row = jnp.repeat(jnp.arange(rows, dtype=F32), GRID_W)
    col = jnp.tile(jnp.arange(GRID_W, dtype=F32), rows)
    inv = ROPE_BASE ** (-jnp.arange(ROPE_PAIRS, dtype=F32) / ROPE_PAIRS)
    lane = jnp.arange(LANE)
    axis = (lane // 32) % 2
    pos = jnp.where(axis[None, :] == 0, row[:, None], col[:, None])
    ang = pos * inv[lane % 16][None, :]
    sign = jnp.where(lane % 32 < 16, -1.0, 1.0)[None, :]
    cos_t = jnp.concatenate([jnp.ones((tc, LANE), F32), jnp.cos(ang)], 0)
    sin_t = jnp.concatenate([jnp.zeros((tc, LANE), F32), jnp.sin(ang) * sign], 0)
    return cos_t, sin_t


def _gdn_kernel(qf_ref, kf_ref, vf_ref, gf_ref, qb_ref, kb_ref, vb_ref, gb_ref, prm_ref,
                of_ref, ob_ref, s_ref, *, chunk):
    c = chunk
    nh = GDN_HEADS

    @pl.when(pl.program_id(1) == 0)
    def _():
        s_ref[...] = jnp.zeros_like(s_ref)

    ii = lax.broadcasted_iota(jnp.int32, (c, c), 0)
    jj = lax.broadcasted_iota(jnp.int32, (c, c), 1)
    neg_a = prm_ref[0:1, :]
    dt_b = prm_ref[1:2, :]
    blk = [(ii >> sh) == (jj >> sh) for sh in range(3, int(math.log2(c)) + 1)]
    dirs = ((qf_ref, kf_ref, vf_ref, gf_ref, of_ref), (qb_ref, kb_ref, vb_ref, gb_ref, ob_ref))
    offm = [blk[lvl] & ~blk[lvl - 1] for lvl in range(1, len(blk))]

    units = []
    for d, (q_ref, k_ref, v_ref, g_ref, o_ref) in enumerate(dirs):
        incl = (ii >= jj) if d == 0 else (ii <= jj)
        strict = (ii > jj) if d == 0 else (ii < jj)
        raw = g_ref[0]
        g_all = neg_a * _softplus(raw + dt_b)
        beta_all = _sigmoid(raw)
        tri = incl.astype(F32)
        cum = _dot(tri, g_all, precision=lax.Precision.HIGHEST)
        cum_t = _dot(g_all.T, tri.T, precision=lax.Precision.HIGHEST)
        last = c - 1 if d == 0 else 0
        for h in range(nh):
            cb = d * nh + h
            cg = 2 * nh + cb
            sl = slice(h * GDN_D, (h + 1) * GDN_D)
            cc = cum[:, cg:cg + 1]
            units.append(dict(
                cb=cb, sl=sl, o_ref=o_ref, incl=incl, strict=strict, cc=cc,
                beta=beta_all[:, cb:cb + 1], cr=cum_t[cg:cg + 1, :], tot=cum[last:last + 1, cg:cg + 1],
                q=q_ref[0, :, sl], k=k_ref[0, :, sl], v=v_ref[0, :, sl]))

    for un in units:
        un["kbeta"] = un["k"].astype(F32) * un["beta"]
        un["gram"] = _dot_nt(jnp.concatenate([un["kbeta"].astype(BF16), un["q"]], axis=0), un["k"])
    for un in units:
        decay = jnp.exp(jnp.where(un["incl"], un["cc"] - un["cr"], -1e30))
        un["low"] = jnp.where(un["strict"], un["gram"][:c] * decay, 0.0)
        un["attn"] = (un["gram"][c:] * decay).astype(BF16)
        un["p"] = jnp.where(blk[0], -un["low"], 0.0)
        un["n"] = un["p"]
    for _ in range(2):
        for un in units:
            pb = un["p"].astype(BF16)
            un["p"] = _dot(pb, pb)
        for un in units:
            un["n"] = un["n"] + un["p"] + _dot(un["n"].astype(BF16), un["p"].astype(BF16))
    for om in offm:
        for un in units:
            off = jnp.where(om, un["low"], 0.0).astype(BF16)
            un["m1"] = off + _dot(off, un["n"].astype(BF16))
        for un in units:
            un["n"] = un["n"] - un["m1"] - _dot(un["n"].astype(BF16), un["m1"].astype(BF16))
    for un in units:
        eg = jnp.exp(un["cc"])
        rhs = jnp.concatenate([un["v"].astype(F32) * un["beta"], un["kbeta"] * eg], axis=1)
        sol = rhs + _dot(un["n"].astype(BF16), rhs.astype(BF16))
        un["u"] = sol[:, :GDN_D]
        qd = un["q"].astype(F32) * eg
        un["wq"] = jnp.concatenate([sol[:, GDN_D:].astype(BF16), qd.astype(BF16)], axis=0)
        un["kdt"] = (un["k"].astype(F32) * jnp.exp(un["tot"] - un["cc"])).T.astype(BF16)
    for un in units:
        un["s"] = s_ref[un["cb"]]
        un["ws"] = _dot(un["wq"], un["s"].astype(BF16))
    for un in units:
        un["v_new"] = (un["u"] - un["ws"][:c]).astype(BF16)
        un["o_ref"][0, :, un["sl"]] = un["ws"][c:] + _dot(un["attn"], un["v_new"])
    for un in units:
        s_ref[un["cb"]] = un["s"] * jnp.exp(un["tot"]) + _dot(un["kdt"], un["v_new"])


def _gdn(qkv, ba, prm, tc, chunk):
    b, l, _ = qkv.shape
    nc = l // chunk
    ncc = tc // chunk

    def fwd(col):
        return lambda bi, n: (bi, n, col)

    def bwd(col):
        return lambda bi, n: (bi, jnp.where(n < ncc, ncc - 1 - n, nc - 1 + ncc - n), col)

    blk = (1, chunk, GDN_W)
    gblk = (1, chunk, LANE)
    out = jax.ShapeDtypeStruct((b, l, GDN_W), F32)
    return pl.pallas_call(
        functools.partial(_gdn_kernel, chunk=chunk),
        grid=(b, nc),
        in_specs=[pl.BlockSpec(blk, fwd(0)), pl.BlockSpec(blk, fwd(1)), pl.BlockSpec(blk, fwd(2)),
                  pl.BlockSpec(gblk, fwd(0)),
                  pl.BlockSpec(blk, bwd(0)), pl.BlockSpec(blk, bwd(1)), pl.BlockSpec(blk, bwd(2)),
                  pl.BlockSpec(gblk, bwd(0)),
                  pl.BlockSpec((8, LANE), lambda bi, n: (0, 0))],
        out_specs=[pl.BlockSpec(blk, fwd(0)), pl.BlockSpec(blk, bwd(0))],
        out_shape=[out, out],
        scratch_shapes=[pltpu.VMEM((2 * GDN_HEADS, GDN_D, GDN_D), F32)],
        compiler_params=_cparams(("arbitrary", "arbitrary")),
        name="gdn",
    )(qkv, qkv, qkv, ba, qkv, qkv, qkv, ba, prm)


def _attn_kernel(qt_ref, k_ref, vt_ref, lp_ref, g_ref, o_ref, s_sc, acc_sc, *, tq, tk, nk_ctx, nk_all, lam_init):
    qt = qt_ref[0, 0]
    sub = lax.broadcasted_iota(jnp.int32, qt.shape, 0)
    zero = jnp.zeros_like(qt)
    qq = jnp.concatenate([jnp.where(sub < DIFF_DH, qt, zero), jnp.where(sub >= DIFF_DH, qt, zero)], axis=1)
    lp = lp_ref[...]
    lam = (jnp.exp(jnp.sum(lp[0:1] * lp[1:2], axis=1, keepdims=True))
           - jnp.exp(jnp.sum(lp[2:3] * lp[3:4], axis=1, keepdims=True)) + lam_init)

    def run(nkv):
        n = nkv * tk
        s_sc[0:n, :] = _dot(k_ref[0, 0:n, :], qq)
        m = jnp.full((1, 2 * tq), -jnp.inf, F32)
        for c in range(nkv):
            s = s_sc[c * tk:(c + 1) * tk, :]
            m_new = jnp.maximum(m, jnp.max(s, axis=0, keepdims=True))
            p = jnp.exp(s - m_new).astype(BF16)
            pv = _dot(vt_ref[0, 0, :, c * tk:(c + 1) * tk], p)
            acc_sc[...] = pv if c == 0 else jnp.exp(m - m_new) * acc_sc[...] + pv
            m = m_new
        on = acc_sc[0:DIFF_DV, :] / acc_sc[DIFF_DV:DIFF_DV + 1, :]
        o = on[:, :tq] - lam * on[:, tq:]
        y = o * lax.rsqrt(jnp.mean(o * o, axis=0, keepdims=True) + EPS) * g_ref[...] * (1.0 - lam_init)
        o_ref[0] = y.T.astype(o_ref.dtype)

    is_ctx = pl.program_id(2) == 0

    @pl.when(is_ctx)
    def _():
        run(nk_ctx)

    @pl.when(jnp.logical_not(is_ctx))
    def _():
        run(nk_all)


def _diff_attn(qt, k, vt, lam_params, norm_g, lam_init, tc):
    b, l, _ = k.shape
    tq = ROW_TILE
    tk = 256
    kern = functools.partial(_attn_kernel, tq=tq, tk=tk, nk_ctx=tc // tk, nk_all=l // tk, lam_init=lam_init)
    return pl.pallas_call(
        kern,
        grid=(b, DIFF_HEADS, l // tq),
        in_specs=[pl.BlockSpec((1, 1, LANE, tq), lambda bi, h, i: (bi, h, 0, i)),
                  pl.BlockSpec((1, l, LANE), lambda bi, h, i: (bi, 0, h)),
                  pl.BlockSpec((1, 1, VT_ROWS, l), lambda bi, h, i: (bi, h, 0, 0)),
                  pl.BlockSpec((4, DIFF_DH), lambda bi, h, i: (0, 0)),
                  pl.BlockSpec((DIFF_DV, 1), lambda bi, h, i: (0, 0))],
        out_specs=pl.BlockSpec((1, tq, LANE), lambda bi, h, i: (bi, i, h)),
        out_shape=jax.ShapeDtypeStruct((b, l, DIFF_W), BF16),
        scratch_shapes=[pltpu.VMEM((l, 2 * tq), F32), pltpu.VMEM((VT_ROWS, 2 * tq), F32)],
        compiler_params=_cparams(("arbitrary", "arbitrary", "arbitrary")),
        name="diff_attn",
    )(qt, k, vt, lam_params, norm_g.reshape(DIFF_DV, 1))


def _outproj_kernel(of_ref, ob_ref, z_ref, yb_ref, yc_ref, w_ref, h_ref, m_ref, gn_ref, g2_ref, wr_ref,
                    h_out, a_out, aff_out):
    o = of_ref[...] + ob_ref[...]
    gn = gn_ref[...]
    parts = []
    for hd in range(GDN_HEADS):
        sl = slice(hd * GDN_D, (hd + 1) * GDN_D)
        parts.append((_rms(o[:, sl], gn) * _silu(z_ref[:, sl].astype(F32))).astype(BF16))
    ya = jnp.concatenate(parts, axis=1)
    acc = _dot(ya, w_ref[0:GDN_W, :])
    acc += _dot(yb_ref[...], w_ref[GDN_W:GDN_W + SC_W, :])
    acc += _dot(yc_ref[...], w_ref[GDN_W + SC_W:, :])
    m = m_ref[0]
    hn = h_ref[...] + m[2:3] * acc
    h_out[...] = hn
    a = _rms(hn, g2_ref[...]) * (1.0 + m[4:5]) + m[3:4]
    a_out[...] = a
    a_hi = a.astype(BF16)
    a_lo = (a - a_hi.astype(F32)).astype(BF16)
    wr = wr_ref[...]
    w_hi = wr.astype(BF16)
    w_lo = (wr - w_hi.astype(F32)).astype(BF16)
    logits = _dot(a_hi, w_hi) + _dot(a_lo, w_hi) + _dot(a_hi, w_lo)
    e = jnp.exp(logits - jnp.max(logits, axis=-1, keepdims=True))
    aff_out[...] = e / jnp.sum(e, axis=-1, keepdims=True)


def _outproj(o_f, o_b, u, y_b, y_c, w_out, h, mods, gdn_g, g2, w_router, tiles_per_sample):
    r, d = h.shape
    tm = ROW_TILE
    zb = COL_Z // GDN_W
    row = lambda i: (i, 0)
    const = lambda i: (0, 0)
    return pl.pallas_call(
        _outproj_kernel,
        grid=(r // tm,),
        in_specs=[pl.BlockSpec((tm, GDN_W), row), pl.BlockSpec((tm, GDN_W), row),
                  pl.BlockSpec((tm, GDN_W), lambda i: (i, zb)),
                  pl.BlockSpec((tm, SC_W), row), pl.BlockSpec((tm, DIFF_W), row),
                  pl.BlockSpec((d, d), const), pl.BlockSpec((tm, d), row),
                  pl.BlockSpec((1, 6, d), _mod_row_map(tiles_per_sample)),
                  pl.BlockSpec((1, GDN_D), const), pl.BlockSpec((1, d), const),
                  pl.BlockSpec((d, N_EXPERTS), const)],
        out_specs=[pl.BlockSpec((tm, d), row), pl.BlockSpec((tm, d), row),
                   pl.BlockSpec((tm, N_EXPERTS), row)],
        out_shape=[jax.ShapeDtypeStruct((r, d), F32), jax.ShapeDtypeStruct((r, d), F32),
                   jax.ShapeDtypeStruct((r, N_EXPERTS), F32)],
        compiler_params=_cparams(("arbitrary",)),
        name="out_proj",
    )(o_f, o_b, u, y_b, y_c, w_out, h, mods, gdn_g.reshape(1, GDN_D), g2.reshape(1, d), w_router)


def _moe_kernel(idx_ref, dst_ref, a_hbm, gate_ref, wg_ref, wu_ref, wd_ref, y_hbm, x32, xb, acc, gsem, ssem,
                *, nrows, nf):
    e = pl.program_id(0)
    f = pl.program_id(1)
    ne = pl.num_programs(0)
    slot = e % 2

    def gather(expert):
        def issue(r, carry):
            pltpu.make_async_copy(a_hbm.at[pl.ds(idx_ref[expert, r], 1), :], x32.at[pl.ds(r, 1), :],
                                  gsem.at[0]).start()
            return carry
        lax.fori_loop(0, nrows, issue, 0, unroll=8)

    def scatter_wait(s):
        pltpu.make_async_copy(acc.at[s], y_hbm.at[pl.ds(0, nrows), :], ssem.at[s]).wait()

    @pl.when((e == 0) & (f == 0))
    def _():
        gather(0)

    @pl.when(f == 0)
    def _():
        pltpu.make_async_copy(a_hbm.at[pl.ds(0, nrows), :], x32, gsem.at[0]).wait()
        xb[...] = x32[...].astype(BF16)

    @pl.when((f == 0) & (e + 1 < ne))
    def _():
        gather(e + 1)

    @pl.when((f == 0) & (e >= 2))
    def _():
        scatter_wait(slot)

    x = xb[...]
    hid = _silu(_dot(x, wg_ref[0, 0].astype(BF16))) * _dot(x, wu_ref[0, 0].astype(BF16))
    part = _dot(hid.astype(BF16), wd_ref[0, 0].astype(BF16))

    @pl.when(f == 0)
    def _():
        acc[slot] = part

    @pl.when(f > 0)
    def _():
        acc[slot] += part

    @pl.when(f == nf - 1)
    def _():
        acc[slot] = acc[slot] * gate_ref[0]

        def issue(r, carry):
            pltpu.make_async_copy(acc.at[slot, pl.ds(r, 1), :], y_hbm.at[pl.ds(dst_ref[e, r], 1), :],
                                  ssem.at[slot]).start()
            return carry
        lax.fori_loop(0, nrows, issue, 0, unroll=8)

    @pl.when((f == nf - 1) & (e == ne - 1))
    def _():
        if ne >= 2:
            scatter_wait(1 - slot)
        scatter_wait(slot)


def _moe(idx, dst, a, gate, w_gate, w_up, w_down, li):
    ne, nrows = idx.shape
    d = a.shape[1]
    ff = w_gate.shape[3]
    tf = 256
    nf = ff // tf
    grid_spec = pltpu.PrefetchScalarGridSpec(
        num_scalar_prefetch=2,
        grid=(ne, nf),
        in_specs=[pl.BlockSpec(memory_space=pl.ANY),
                  pl.BlockSpec((1, nrows, 1), lambda e, f, i_ref, d_ref: (e, 0, 0)),
                  pl.BlockSpec((1, 1, d, tf), lambda e, f, i_ref, d_ref: (li, e, 0, f)),
                  pl.BlockSpec((1, 1, d, tf), lambda e, f, i_ref, d_ref: (li, e, 0, f)),
                  pl.BlockSpec((1, 1, tf, d), lambda e, f, i_ref, d_ref: (li, e, f, 0))],
        out_specs=pl.BlockSpec(memory_space=pl.ANY),
        scratch_shapes=[pltpu.VMEM((nrows, d), F32), pltpu.VMEM((nrows, d), BF16),
                        pltpu.VMEM((2, nrows, d), F32),
                        pltpu.SemaphoreType.DMA((1,)), pltpu.SemaphoreType.DMA((2,))],
    )
    return pl.pallas_call(
        functools.partial(_moe_kernel, nrows=nrows, nf=nf),
        grid_spec=grid_spec,
        out_shape=jax.ShapeDtypeStruct((ne * nrows, d), F32),
        compiler_params=_cparams(("arbitrary", "arbitrary")),
        name="moe_experts",
    )(idx, dst, a, gate, w_gate, w_up, w_down)


COMBINE_TILE = 128
COMBINE_CHUNK = 256


def _combine_kernel(ts_ref, h_ref, y_hbm, st_ref, cn_ref, m_ref, mn_ref, g_ref, *rest, final, smax):
    if final:
        o_ref, stage, acc, sem = rest
    else:
        h_out, a_out, stage, acc, sem = rest
    i = pl.program_id(0)
    tt = h_ref.shape[0]
    s = pl.multiple_of(ts_ref[0, i] & ~7, 8)
    n = ts_ref[1, i] + (ts_ref[0, i] & 7)

    @pl.when(i == 0)
    def _():
        stage[...] = jnp.zeros_like(stage)

    pieces = [(1 << k, (n & (1 << k)) != 0, pl.multiple_of(n & ~((2 << k) - 1), 8))
              for k in range(int(math.log2(smax)), 2, -1)]
    pieces += [(1, j < (n & 7), (n & ~7) + j) for j in range(7)]

    def piece(p, off):
        src = s + off if p == 1 else pl.multiple_of(s + off, 8)
        return pltpu.make_async_copy(y_hbm.at[pl.ds(src, p), :], stage.at[pl.ds(off, p), :], sem.at[0])

    for p, cond, off in pieces:
        @pl.when(cond)
        def _(p=p, off=off):
            piece(p, off).start()
    for p, cond, off in pieces:
        @pl.when(cond)
        def _(p=p, off=off):
            piece(p, off).wait()

    acc[...] = jnp.zeros_like(acc)
    lo = st_ref[...] - s
    hi = lo + cn_ref[...]

    def body(c, carry):
        base = pl.multiple_of(c * COMBINE_CHUNK, COMBINE_CHUNK)
        jj = base + lax.broadcasted_iota(jnp.int32, (tt, COMBINE_CHUNK), 1)
        seg = jnp.where((jj >= lo) & (jj < hi), 1.0, 0.0).astype(BF16)
        acc[...] += _dot(seg, stage[pl.ds(base, COMBINE_CHUNK), :].astype(BF16))
        return carry

    lax.fori_loop(0, (n + COMBINE_CHUNK - 1) // COMBINE_CHUNK, body, 0)
    hn = h_ref[...] + m_ref[0][5:6] * acc[...]
    if final:
        o_ref[...] = _rms(hn, g_ref[...])
    else:
        h_out[...] = hn
        mn = mn_ref[0]
        a_out[...] = (_rms(hn, g_ref[...]) * (1.0 + mn[1:2]) + mn[0:1]).astype(a_out.dtype)


def _combine(h, y, tile_start, start, count, mods, mods_next, g, tiles_per_sample, final):
    r, d = h.shape
    tt = COMBINE_TILE
    smax = N_EXPERTS * tt
    per = tiles_per_sample * (ROW_TILE // tt)
    row = lambda i, ts: (i, 0)
    const = lambda i, ts: (0, 0)

    def mod_map(i, ts):
        return (jnp.where(i % per < ROW_TILE // tt, 2, i // per), 0, 0)

    if final:
        out_specs = pl.BlockSpec((tt, d), row)
        out_shape = jax.ShapeDtypeStruct((r, d), F32)
    else:
        out_specs = [pl.BlockSpec((tt, d), row), pl.BlockSpec((tt, d), row)]
        out_shape = [jax.ShapeDtypeStruct((r, d), F32), jax.ShapeDtypeStruct((r, d), BF16)]
    grid_spec = pltpu.PrefetchScalarGridSpec(
        num_scalar_prefetch=1,
        grid=(r // tt,),
        in_specs=[pl.BlockSpec((tt, d), row), pl.BlockSpec(memory_space=pl.ANY),
                  pl.BlockSpec((tt, 1), row), pl.BlockSpec((tt, 1), row),
                  pl.BlockSpec((1, 6, d), mod_map), pl.BlockSpec((1, 6, d), mod_map),
                  pl.BlockSpec((1, d), const)],
        out_specs=out_specs,
        scratch_shapes=[pltpu.VMEM((smax + COMBINE_CHUNK, d), F32), pltpu.VMEM((tt, d), F32),
                        pltpu.SemaphoreType.DMA((1,))],
    )
    return pl.pallas_call(
        functools.partial(_combine_kernel, final=final, smax=smax),
        grid_spec=grid_spec,
        out_shape=out_shape,
        compiler_params=_cparams(("arbitrary",)),
        name="moe_combine",
    )(tile_start, h, y, start, count, mods, mods_next, g.reshape(1, d))


def _route(aff, b, l, tc, with_ctx):
    aff3 = aff.reshape(b, l, N_EXPERTS)
    ne = N_EXPERTS
    ids, gates = [], []
    sel = jnp.zeros((b, l, ne), jnp.bool_)
    segs = [(tc, l - tc)] + ([(0, tc)] if with_ctx else [])
    for start, n in segs:
        cap = EC_CAPACITY * n // ne
        seg_aff = jnp.swapaxes(aff3[:, start:start + n], 1, 2)
        _, idx = lax.top_k(seg_aff, cap)
        idx = jnp.sort(idx, axis=-1)
        g = jnp.take_along_axis(seg_aff, idx, axis=-1)
        hit = jnp.any(idx[..., None] == jnp.arange(n, dtype=idx.dtype), axis=2)
        sel = sel.at[:, start:start + n].set(jnp.swapaxes(hit, 1, 2))
        rows = idx + start + (jnp.arange(b, dtype=jnp.int32) * l)[:, None, None]
        ids.append(jnp.swapaxes(rows, 0, 1).reshape(ne, b * cap))
        gates.append(jnp.swapaxes(g, 0, 1).reshape(ne, b * cap))
    idx_all = jnp.concatenate(ids, 1).astype(jnp.int32)
    gate_all = jnp.concatenate(gates, 1)[..., None]
    sel_i = sel.reshape(b * l, ne).astype(jnp.int32)
    count = jnp.sum(sel_i, axis=1)
    start_row = jnp.cumsum(count) - count
    slot = start_row[:, None] + jnp.cumsum(sel_i, axis=1) - sel_i
    dst_all = jnp.take_along_axis(slot.T, idx_all, axis=1).astype(jnp.int32)
    tile_n = jnp.sum(count.reshape(-1, COMBINE_TILE), axis=1)
    tile_info = jnp.stack([start_row[::COMBINE_TILE], tile_n]).astype(jnp.int32)
    return idx_all, gate_all, dst_all, tile_info, start_row[:, None].astype(jnp.int32), \
        count[:, None].astype(jnp.int32)


def kernel(x, c, ctx, c_ctx, w_mod, b_mod, norm1_g, norm2_g, w_in, gdn_conv_w, gdn_a_log, gdn_dt_bias,
           gdn_norm_g, sc_conv_w, diff_lambda, diff_norm_g, w_out, w_router, w_e_gate, w_e_up, w_e_down,
           final_norm_g):
    b, t, d = x.shape
    tc = ctx.shape[1]
    l = tc + t
    r = b * l
    depth = w_mod.shape[0]
    tiles = l // ROW_TILE
    nh = GDN_HEADS

    cond = jnp.concatenate([c, c_ctx[None, :], jnp.zeros((8 - b - 1, d), F32)], 0)
    mods_all = _modulation(cond, w_mod, b_mod)[:, :3].reshape(depth, 3, 6, d)
    cos_t, sin_t = _rope_tables(t, tc)

    h = jnp.concatenate([ctx, x], 1).reshape(r, d)
    a = _norm_mod(h, norm1_g[0], mods_all[0], tiles)
    out = None
    for li in range(depth):
        with_ctx = li < depth - 1
        lam_init = 0.8 - 0.6 * math.exp(-0.3 * li)
        wl = w_in[li]
        a_in = 4 * GDN_W
        w_main = jnp.concatenate([wl[:, :a_in], wl[:, a_in + 4 * nh:]], 1).astype(BF16)
        w_ba = jnp.concatenate([wl[:, a_in:a_in + 4 * nh], jnp.zeros((d, LANE - 4 * nh), F32)], 1).astype(BF16)
        u = _matmul(a, w_main, BF16, r // 8, GDN_W)
        ba = _matmul(a, w_ba, F32, r // 8, LANE)
        u3 = u.reshape(b, l, N_MAIN)

        qkv = _gdn_prep(u3, gdn_conv_w[li], tc)
        prm = jnp.zeros((8, LANE), F32)
        prm = prm.at[0, 2 * nh:4 * nh].set(-jnp.exp(gdn_a_log[li].reshape(-1)))
        prm = prm.at[1, 2 * nh:4 * nh].set(gdn_dt_bias[li].reshape(-1))
        o_f, o_b = _gdn(qkv, ba.reshape(b, l, LANE), prm, tc, GDN_CHUNK)
        y_b = _short_conv(u3, sc_conv_w[li], tc)
        q_t, k_r, v_t = _attn_prep(u3, cos_t, sin_t)
        y_c = _diff_attn(q_t, k_r, v_t, diff_lambda[li], diff_norm_g[li], lam_init, tc)

        h, a2, aff = _outproj(o_f.reshape(r, GDN_W), o_b.reshape(r, GDN_W), u, y_b.reshape(r, SC_W),
                              y_c.reshape(r, DIFF_W), w_out[li].astype(BF16), h, mods_all[li],
                              gdn_norm_g[li], norm2_g[li], w_router[li], tiles)
        idx, gate, dst, tile_start, start_row, count = _route(aff, b, l, tc, with_ctx)
        y = _moe(idx, dst, a2, gate, w_e_gate, w_e_up, w_e_down, li)
        if with_ctx:
            h, a = _combine(h, y, tile_start, start_row, count, mods_all[li], mods_all[li + 1],
                            norm1_g[li + 1], tiles, final=False)
        else:
            out = _combine(h, y, tile_start, start_row, count, mods_all[li], mods_all[li],
                           final_norm_g, tiles, final=True)
    return out.reshape(b, l, d)[:, tc:]
```

```python
import functools
import math

import jax
import jax.numpy as jnp
from jax import lax
from jax.experimental import pallas as pl
from jax.experimental.pallas import tpu as pltpu

F32 = jnp.float32
BF16 = jnp.bfloat16

EPS = 1e-6
GRID_W = 64
GDN_HEADS = 6
GDN_D = 128
GDN_W = GDN_HEADS * GDN_D
SC_W = 512
DIFF_HEADS = 6
DIFF_DH = 64
DIFF_DV = 128
DIFF_W = DIFF_HEADS * DIFF_DV
ROPE_BASE = 10000.0
ROPE_PAIRS = DIFF_DH // 4
N_EXPERTS = 16
EC_CAPACITY = 2

COL_Z = 3 * GDN_W
COL_SC = COL_Z + GDN_W
COL_ATT = COL_SC + 3 * SC_W
N_MAIN = COL_ATT + 3 * DIFF_W
LANE = 128
ROW_TILE = 256
GDN_CHUNK = 128
VT_ROWS = DIFF_DV + 16
VMEM_LIMIT = 56 * 1024 * 1024


def _cparams(sem, vmem=VMEM_LIMIT):
    return pltpu.CompilerParams(dimension_semantics=sem, vmem_limit_bytes=vmem)


def _sigmoid(x):
    return 1.0 / (1.0 + jnp.exp(-x))


def _silu(x):
    return x * _sigmoid(x)


def _softplus(x):
    return jnp.maximum(x, 0.0) + jnp.log(1.0 + jnp.exp(-jnp.abs(x)))


def _rms(x, g):
    return x * lax.rsqrt(jnp.mean(x * x, axis=-1, keepdims=True) + EPS) * g


def _dot(a, b, **kw):
    return jnp.dot(a, b, preferred_element_type=F32, **kw)


def _dot_nt(a, b):
    return lax.dot_general(a, b, (((1,), (1,)), ((), ())), preferred_element_type=F32)


def _mod_kernel(s_ref, w_ref, b_ref, o_ref):
    s = _silu(s_ref[...]).astype(BF16)
    o_ref[0] = _dot(s, w_ref[0].astype(BF16)) + b_ref[0]


def _modulation(cond, w_mod, b_mod):
    depth, d, n = w_mod.shape
    tn = 1536
    return pl.pallas_call(
        _mod_kernel,
        grid=(depth, n // tn),
        in_specs=[pl.BlockSpec((8, d), lambda l, j: (0, 0)),
                  pl.BlockSpec((1, d, tn), lambda l, j: (l, 0, j)),
                  pl.BlockSpec((1, 1, tn), lambda l, j: (l, 0, j))],
        out_specs=pl.BlockSpec((1, 8, tn), lambda l, j: (l, 0, j)),
        out_shape=jax.ShapeDtypeStruct((depth, 8, n), F32),
        compiler_params=_cparams(("arbitrary", "arbitrary")),
        name="modulation",
    )(cond, w_mod, b_mod.reshape(depth, 1, n))


def _mod_row_map(tiles_per_sample):
    def index_map(i):
        return (jnp.where(i % tiles_per_sample == 0, 2, i // tiles_per_sample), 0, 0)
    return index_map


def _norm_mod_kernel(h_ref, g_ref, m_ref, a_ref):
    m = m_ref[0]
    a = _rms(h_ref[...], g_ref[...]) * (1.0 + m[1:2]) + m[0:1]
    a_ref[...] = a.astype(a_ref.dtype)


def _norm_mod(h, g, mods, tiles_per_sample):
    r, d = h.shape
    return pl.pallas_call(
        _norm_mod_kernel,
        grid=(r // ROW_TILE,),
        in_specs=[pl.BlockSpec((ROW_TILE, d), lambda i: (i, 0)),
                  pl.BlockSpec((1, d), lambda i: (0, 0)),
                  pl.BlockSpec((1, 6, d), _mod_row_map(tiles_per_sample))],
        out_specs=pl.BlockSpec((ROW_TILE, d), lambda i: (i, 0)),
        out_shape=jax.ShapeDtypeStruct((r, d), BF16),
        compiler_params=_cparams(("arbitrary",)),
        name="norm_mod",
    )(h, g.reshape(1, d), mods)


def _mm_kernel(a_ref, w_ref, o_ref):
    o_ref[...] = _dot(a_ref[...], w_ref[...]).astype(o_ref.dtype)


def _matmul(a, w, out_dtype, tm, tn):
    r, k = a.shape
    n = w.shape[1]
    return pl.pallas_call(
        _mm_kernel,
        grid=(n // tn, r // tm),
        in_specs=[pl.BlockSpec((tm, k), lambda j, i: (i, 0)),
                  pl.BlockSpec((k, tn), lambda j, i: (0, j))],
        out_specs=pl.BlockSpec((tm, tn), lambda j, i: (i, j)),
        out_shape=jax.ShapeDtypeStruct((r, n), out_dtype),
        compiler_params=_cparams(("arbitrary", "arbitrary")),
        name="in_proj",
    )(a, w)


def _conv3(x, w, tc):
    l = x.shape[0]
    row = lax.broadcasted_iota(jnp.int32, x.shape, 0)
    prev = jnp.where((row == 0) | (row == tc), 0.0, pltpu.roll(x, 1, 0))
    nxt = jnp.where((row == tc - 1) | (row == l - 1), 0.0, pltpu.roll(x, l - 1, 0))
    return prev * w[0:1] + x * w[1:2] + nxt * w[2:3]


def _gdn_prep_kernel(u_ref, w_ref, o_ref, *, tc):
    j = pl.program_id(1)
    y = _silu(_conv3(u_ref[0].astype(F32), w_ref[...], tc))
    inv = lax.rsqrt(jnp.sum(y * y, axis=-1, keepdims=True) + EPS)
    nh = GDN_HEADS
    fac = jnp.where(j < nh, inv * GDN_D ** -0.5, jnp.where(j < 2 * nh, inv, 1.0))
    o_ref[0] = (y * fac).astype(o_ref.dtype)


def _gdn_prep(u3, conv_w, tc):
    b, l, _ = u3.shape
    nblk = 3 * GDN_W // LANE
    return pl.pallas_call(
        functools.partial(_gdn_prep_kernel, tc=tc),
        grid=(b, nblk),
        in_specs=[pl.BlockSpec((1, l, LANE), lambda bi, j: (bi, 0, j)),
                  pl.BlockSpec((3, LANE), lambda bi, j: (0, j))],
        out_specs=pl.BlockSpec((1, l, LANE), lambda bi, j: (bi, 0, j)),
        out_shape=jax.ShapeDtypeStruct((b, l, 3 * GDN_W), BF16),
        compiler_params=_cparams(("arbitrary", "arbitrary")),
        name="gdn_prep",
    )(u3, conv_w)


def _sconv_kernel(b_ref, c_ref, x_ref, w_ref, o_ref, *, tc):
    inner = c_ref[0].astype(F32) * x_ref[0].astype(F32)
    o_ref[0] = (b_ref[0].astype(F32) * _conv3(inner, w_ref[...], tc)).astype(o_ref.dtype)


def _short_conv(u3, conv_w, tc):
    b, l, _ = u3.shape
    nblk = SC_W // LANE
    base = COL_SC // LANE

    def spec(k):
        return pl.BlockSpec((1, l, LANE), lambda bi, j: (bi, 0, base + k * nblk + j))

    return pl.pallas_call(
        functools.partial(_sconv_kernel, tc=tc),
        grid=(b, nblk),
        in_specs=[spec(0), spec(1), spec(2), pl.BlockSpec((3, LANE), lambda bi, j: (0, j))],
        out_specs=pl.BlockSpec((1, l, LANE), lambda bi, j: (bi, 0, j)),
        out_shape=jax.ShapeDtypeStruct((b, l, SC_W), BF16),
        compiler_params=_cparams(("arbitrary", "arbitrary")),
        name="short_conv",
    )(u3, u3, u3, conv_w)


def _rope(x, cos, sin):
    lane = lax.broadcasted_iota(jnp.int32, x.shape, 1)
    swapped = jnp.where(lane % 32 < 16, pltpu.roll(x, LANE - 16, 1), pltpu.roll(x, 16, 1))
    return x * cos + swapped * sin


def _attn_prep_kernel(q_ref, k_ref, v_ref, cos_ref, sin_ref, qt_ref, ko_ref, vt_ref):
    cos = cos_ref[...]
    sin = sin_ref[...]
    q = _rope(q_ref[0].astype(F32), cos, sin) * DIFF_DH ** -0.5
    qt_ref[0, 0] = q.T.astype(qt_ref.dtype)
    ko_ref[0] = _rope(k_ref[0].astype(F32), cos, sin).astype(ko_ref.dtype)
    vt_ref[0, 0, 0:DIFF_DV, :] = v_ref[0].astype(F32).T.astype(vt_ref.dtype)
    vt_ref[0, 0, DIFF_DV:, :] = jnp.ones((VT_ROWS - DIFF_DV, v_ref.shape[1]), vt_ref.dtype)


def _attn_prep(u3, cos_t, sin_t):
    b, l, _ = u3.shape
    nh = DIFF_HEADS
    base = COL_ATT // LANE
    tbl = pl.BlockSpec((l, LANE), lambda bi, j: (0, 0))
    tspec = pl.BlockSpec((1, 1, LANE, l), lambda bi, j: (bi, j, 0, 0))
    tshape = jax.ShapeDtypeStruct((b, nh, LANE, l), BF16)
    vspec = pl.BlockSpec((1, 1, VT_ROWS, l), lambda bi, j: (bi, j, 0, 0))
    vshape = jax.ShapeDtypeStruct((b, nh, VT_ROWS, l), BF16)
    return pl.pallas_call(
        _attn_prep_kernel,
        grid=(b, nh),
        in_specs=[pl.BlockSpec((1, l, LANE), lambda bi, j: (bi, 0, base + j)),
                  pl.BlockSpec((1, l, LANE), lambda bi, j: (bi, 0, base + nh + j)),
                  pl.BlockSpec((1, l, LANE), lambda bi, j: (bi, 0, base + 2 * nh + j)),
                  tbl, tbl],
        out_specs=[tspec, pl.BlockSpec((1, l, LANE), lambda bi, j: (bi, 0, j)), vspec],
        out_shape=[tshape, jax.ShapeDtypeStruct((b, l, DIFF_W), BF16), vshape],
        compiler_params=_cparams(("arbitrary", "arbitrary")),
        name="attn_prep",
    )(u3, u3, u3, cos_t, sin_t)


def _rope_tables(t, tc):
    rows = t // GRID_W
    row = jnp.repeat(jnp.arange(rows, dtype=F32), GRID_W)
    col = jnp.tile(jnp.arange(GRID_W, dtype=F32), rows)
    inv = ROPE_BASE ** (-jnp.arange(ROPE_PAIRS, dtype=F32) / ROPE_PAIRS)
    lane = jnp.arange(LANE)
    axis = (lane // 32) % 2
    pos = jnp.where(axis[None, :] == 0, row[:, None], col[:, None])
    ang = pos * inv[lane % 16][None, :]
    sign = jnp.where(lane % 32 < 16, -1.0, 1.0)[None, :]
    cos_t = jnp.concatenate([jnp.ones((tc, LANE), F32), jnp.cos(ang)], 0)
    sin_t = jnp.concatenate([jnp.zeros((tc, LANE), F32), jnp.sin(ang) * sign], 0)
    return cos_t, sin_t


def _gdn_kernel(qf_ref, kf_ref, vf_ref, gf_ref, qb_ref, kb_ref, vb_ref, gb_ref, prm_ref,
                of_ref, ob_ref, s_ref, *, chunk):
    c = chunk
    nh = GDN_HEADS

    @pl.when(pl.program_id(1) == 0)
    def _():
        s_ref[...] = jnp.zeros_like(s_ref)

    ii = lax.broadcasted_iota(jnp.int32, (c, c), 0)
    jj = lax.broadcasted_iota(jnp.int32, (c, c), 1)
    neg_a = prm_ref[0:1, :]
    dt_b = prm_ref[1:2, :]
    blk = [(ii >> sh) == (jj >> sh) for sh in range(3, int(math.log2(c)) + 1)]
    dirs = ((qf_ref, kf_ref, vf_ref, gf_ref, of_ref), (qb_ref, kb_ref, vb_ref, gb_ref, ob_ref))
    offm = [blk[lvl] & ~blk[lvl - 1] for lvl in range(1, len(blk))]

    units = []
    for d, (q_ref, k_ref, v_ref, g_ref, o_ref) in enumerate(dirs):
        incl = (ii >= jj) if d == 0 else (ii <= jj)
        strict = (ii > jj) if d == 0 else (ii < jj)
        raw = g_ref[0]
        g_all = neg_a * _softplus(raw + dt_b)
        beta_all = _sigmoid(raw)
        tri = incl.astype(F32)
        cum = _dot(tri, g_all, precision=lax.Precision.HIGHEST)
        cum_t = _dot(g_all.T, tri.T, precision=lax.Precision.HIGHEST)
        last = c - 1 if d == 0 else 0
        for h in range(nh):
            cb = d * nh + h
            cg = 2 * nh + cb
            sl = slice(h * GDN_D, (h + 1) * GDN_D)
            cc = cum[:, cg:cg + 1]
            units.append(dict(
                cb=cb, sl=sl, o_ref=o_ref, incl=incl, strict=strict, cc=cc,
                beta=beta_all[:, cb:cb + 1], cr=cum_t[cg:cg + 1, :], tot=cum[last:last + 1, cg:cg + 1],
                q=q_ref[0, :, sl], k=k_ref[0, :, sl], v=v_ref[0, :, sl]))

    for un in units:
        un["kbeta"] = un["k"].astype(F32) * un["beta"]
        un["gram"] = _dot_nt(jnp.concatenate([un["kbeta"].astype(BF16), un["q"]], axis=0), un["k"])
    for un in units:
        decay = jnp.exp(jnp.where(un["incl"], un["cc"] - un["cr"], -1e30))
        un["low"] = jnp.where(un["strict"], un["gram"][:c] * decay, 0.0)
        un["attn"] = (un["gram"][c:] * decay).astype(BF16)
        un["p"] = jnp.where(blk[0], -un["low"], 0.0)
        un["n"] = un["p"]
    for _ in range(2):
        for un in units:
            pb = un["p"].astype(BF16)
            un["p"] = _dot(pb, pb)
        for un in units:
            un["n"] = un["n"] + un["p"] + _dot(un["n"].astype(BF16), un["p"].astype(BF16))
    for om in offm:
        for un in units:
            off = jnp.where(om, un["low"], 0.0).astype(BF16)
            un["m1"] = off + _dot(off, un["n"].astype(BF16))
        for un in units:
            un["n"] = un["n"] - un["m1"] - _dot(un["n"].astype(BF16), un["m1"].astype(BF16))
    for un in units:
        eg = jnp.exp(un["cc"])
        rhs = jnp.concatenate([un["v"].astype(F32) * un["beta"], un["kbeta"] * eg], axis=1)
        sol = rhs + _dot(un["n"].astype(BF16), rhs.astype(BF16))
        un["u"] = sol[:, :GDN_D]
        qd = un["q"].astype(F32) * eg
        un["wq"] = jnp.concatenate([sol[:, GDN_D:].astype(BF16), qd.astype(BF16)], axis=0)
        un["kdt"] = (un["k"].astype(F32) * jnp.exp(un["tot"] - un["cc"])).T.astype(BF16)
    for un in units:
        un["s"] = s_ref[un["cb"]]
        un["ws"] = _dot(un["wq"], un["s"].astype(BF16))
    for un in units:
        un["v_new"] = (un["u"] - un["ws"][:c]).astype(BF16)
        un["o_ref"][0, :, un["sl"]] = un["ws"][c:] + _dot(un["attn"], un["v_new"])
    for un in units:
        s_ref[un["cb"]] = un["s"] * jnp.exp(un["tot"]) + _dot(un["kdt"], un["v_new"])


def _gdn(qkv, ba, prm, tc, chunk):
    b, l, _ = qkv.shape
    nc = l // chunk
    ncc = tc // chunk

    def fwd(col):
        return lambda bi, n: (bi, n, col)

    def bwd(col):
        return lambda bi, n: (bi, jnp.where(n < ncc, ncc - 1 - n, nc - 1 + ncc - n), col)

    blk = (1, chunk, GDN_W)
    gblk = (1, chunk, LANE)
    out = jax.ShapeDtypeStruct((b, l, GDN_W), F32)
    return pl.pallas_call(
        functools.partial(_gdn_kernel, chunk=chunk),
        grid=(b, nc),
        in_specs=[pl.BlockSpec(blk, fwd(0)), pl.BlockSpec(blk, fwd(1)), pl.BlockSpec(blk, fwd(2)),
                  pl.BlockSpec(gblk, fwd(0)),
                  pl.BlockSpec(blk, bwd(0)), pl.BlockSpec(blk, bwd(1)), pl.BlockSpec(blk, bwd(2)),
                  pl.BlockSpec(gblk, bwd(0)),
                  pl.BlockSpec((8, LANE), lambda bi, n: (0, 0))],
        out_specs=[pl.BlockSpec(blk, fwd(0)), pl.BlockSpec(blk, bwd(0))],
        out_shape=[out, out],
        scratch_shapes=[pltpu.VMEM((2 * GDN_HEADS, GDN_D, GDN_D), F32)],
        compiler_params=_cparams(("arbitrary", "arbitrary")),
        name="gdn",
    )(qkv, qkv, qkv, ba, qkv, qkv, qkv, ba, prm)


def _attn_kernel(qt_ref, k_ref, vt_ref, lp_ref, g_ref, o_ref, s_sc, acc_sc, *, tq, tk, nk_ctx, nk_all, lam_init):
    qt = qt_ref[0, 0]
    sub = lax.broadcasted_iota(jnp.int32, qt.shape, 0)
    zero = jnp.zeros_like(qt)
    qq = jnp.concatenate([jnp.where(sub < DIFF_DH, qt, zero), jnp.where(sub >= DIFF_DH, qt, zero)], axis=1)
    lp = lp_ref[...]
    lam = (jnp.exp(jnp.sum(lp[0:1] * lp[1:2], axis=1, keepdims=True))
           - jnp.exp(jnp.sum(lp[2:3] * lp[3:4], axis=1, keepdims=True)) + lam_init)

    def run(nkv):
        n = nkv * tk
        s_sc[0:n, :] = _dot(k_ref[0, 0:n, :], qq)
        m = jnp.full((1, 2 * tq), -jnp.inf, F32)
        for c in range(nkv):
            s = s_sc[c * tk:(c + 1) * tk, :]
            m_new = jnp.maximum(m, jnp.max(s, axis=0, keepdims=True))
            p = jnp.exp(s - m_new).astype(BF16)
            pv = _dot(vt_ref[0, 0, :, c * tk:(c + 1) * tk], p)
            acc_sc[...] = pv if c == 0 else jnp.exp(m - m_new) * acc_sc[...] + pv
            m = m_new
        on = acc_sc[0:DIFF_DV, :] / acc_sc[DIFF_DV:DIFF_DV + 1, :]
        o = on[:, :tq] - lam * on[:, tq:]
        y = o * lax.rsqrt(jnp.mean(o * o, axis=0, keepdims=True) + EPS) * g_ref[...] * (1.0 - lam_init)
        o_ref[0] = y.T.astype(o_ref.dtype)

    is_ctx = pl.program_id(2) == 0

    @pl.when(is_ctx)
    def _():
        run(nk_ctx)

    @pl.when(jnp.logical_not(is_ctx))
    def _():
        run(nk_all)


def _diff_attn(qt, k, vt, lam_params, norm_g, lam_init, tc):
    b, l, _ = k.shape
    tq = ROW_TILE
    tk = 256
    kern = functools.partial(_attn_kernel, tq=tq, tk=tk, nk_ctx=tc // tk, nk_all=l // tk, lam_init=lam_init)
    return pl.pallas_call(
        kern,
        grid=(b, DIFF_HEADS, l // tq),
        in_specs=[pl.BlockSpec((1, 1, LANE, tq), lambda bi, h, i: (bi, h, 0, i)),
                  pl.BlockSpec((1, l, LANE), lambda bi, h, i: (bi, 0, h)),
                  pl.BlockSpec((1, 1, VT_ROWS, l), lambda bi, h, i: (bi, h, 0, 0)),
                  pl.BlockSpec((4, DIFF_DH), lambda bi, h, i: (0, 0)),
                  pl.BlockSpec((DIFF_DV, 1), lambda bi, h, i: (0, 0))],
        out_specs=pl.BlockSpec((1, tq, LANE), lambda bi, h, i: (bi, i, h)),
        out_shape=jax.ShapeDtypeStruct((b, l, DIFF_W), BF16),
        scratch_shapes=[pltpu.VMEM((l, 2 * tq), F32), pltpu.VMEM((VT_ROWS, 2 * tq), F32)],
        compiler_params=_cparams(("arbitrary", "arbitrary", "arbitrary")),
        name="diff_attn",
    )(qt, k, vt, lam_params, norm_g.reshape(DIFF_DV, 1))


def _outproj_kernel(of_ref, ob_ref, z_ref, yb_ref, yc_ref, w_ref, h_ref, m_ref, gn_ref, g2_ref, wr_ref,
                    h_out, a_out, aff_out):
    o = of_ref[...] + ob_ref[...]
    gn = gn_ref[...]
    parts = []
    for hd in range(GDN_HEADS):
        sl = slice(hd * GDN_D, (hd + 1) * GDN_D)
        parts.append((_rms(o[:, sl], gn) * _silu(z_ref[:, sl].astype(F32))).astype(BF16))
    ya = jnp.concatenate(parts, axis=1)
    acc = _dot(ya, w_ref[0:GDN_W, :])
    acc += _dot(yb_ref[...], w_ref[GDN_W:GDN_W + SC_W, :])
    acc += _dot(yc_ref[...], w_ref[GDN_W + SC_W:, :])
    m = m_ref[0]
    hn = h_ref[...] + m[2:3] * acc
    h_out[...] = hn
    a = _rms(hn, g2_ref[...]) * (1.0 + m[4:5]) + m[3:4]
    a_out[...] = a
    a_hi = a.astype(BF16)
    a_lo = (a - a_hi.astype(F32)).astype(BF16)
    wr = wr_ref[...]
    w_hi = wr.astype(BF16)
    w_lo = (wr - w_hi.astype(F32)).astype(BF16)
    logits = _dot(a_hi, w_hi) + _dot(a_lo, w_hi) + _dot(a_hi, w_lo)
    e = jnp.exp(logits - jnp.max(logits, axis=-1, keepdims=True))
    aff_out[...] = e / jnp.sum(e, axis=-1, keepdims=True)


def _outproj(o_f, o_b, u, y_b, y_c, w_out, h, mods, gdn_g, g2, w_router, tiles_per_sample):
    r, d = h.shape
    tm = ROW_TILE
    zb = COL_Z // GDN_W
    row = lambda i: (i, 0)
    const = lambda i: (0, 0)
    return pl.pallas_call(
        _outproj_kernel,
        grid=(r // tm,),
        in_specs=[pl.BlockSpec((tm, GDN_W), row), pl.BlockSpec((tm, GDN_W), row),
                  pl.BlockSpec((tm, GDN_W), lambda i: (i, zb)),
                  pl.BlockSpec((tm, SC_W), row), pl.BlockSpec((tm, DIFF_W), row),
                  pl.BlockSpec((d, d), const), pl.BlockSpec((tm, d), row),
                  pl.BlockSpec((1, 6, d), _mod_row_map(tiles_per_sample)),
                  pl.BlockSpec((1, GDN_D), const), pl.BlockSpec((1, d), const),
                  pl.BlockSpec((d, N_EXPERTS), const)],
        out_specs=[pl.BlockSpec((tm, d), row), pl.BlockSpec((tm, d), row),
                   pl.BlockSpec((tm, N_EXPERTS), row)],
        out_shape=[jax.ShapeDtypeStruct((r, d), F32), jax.ShapeDtypeStruct((r, d), F32),
                   jax.ShapeDtypeStruct((r, N_EXPERTS), F32)],
        compiler_params=_cparams(("arbitrary",)),
        name="out_proj",
    )(o_f, o_b, u, y_b, y_c, w_out, h, mods, gdn_g.reshape(1, GDN_D), g2.reshape(1, d), w_router)


def _moe_kernel(idx_ref, dst_ref, a_hbm, gate_ref, wg_ref, wu_ref, wd_ref, y_hbm, x32, xb, acc, gsem, ssem,
                *, nrows, nf):
    e = pl.program_id(0)
    f = pl.program_id(1)
    ne = pl.num_programs(0)
    slot = e % 2

    def gather(expert):
        def issue(g, carry):
            r0 = pl.multiple_of(g * 8, 8)
            for k in range(8):
                pltpu.make_async_copy(a_hbm.at[pl.ds(idx_ref[expert * nrows + r0 + k], 1), :],
                                      x32.at[pl.ds(r0 + k, 1), :], gsem.at[0]).start()
            return carry
        lax.fori_loop(0, nrows // 8, issue, 0)

    def scatter_wait(s):
        pltpu.make_async_copy(acc.at[s], y_hbm.at[pl.ds(0, nrows), :], ssem.at[s]).wait()

    @pl.when((e == 0) & (f == 0))
    def _():
        gather(0)

    @pl.when(f == 0)
    def _():
        pltpu.make_async_copy(a_hbm.at[pl.ds(0, nrows), :], x32, gsem.at[0]).wait()
        xb[...] = x32[...].astype(BF16)

    @pl.when((f == 0) & (e + 1 < ne))
    def _():
        gather(e + 1)

    @pl.when((f == 0) & (e >= 2))
    def _():
        scatter_wait(slot)

    x = xb[...]
    hid = _silu(_dot(x, wg_ref[0, 0].astype(BF16))) * _dot(x, wu_ref[0, 0].astype(BF16))
    part = _dot(hid.astype(BF16), wd_ref[0, 0].astype(BF16))

    @pl.when(f == 0)
    def _():
        acc[slot] = part

    @pl.when(f > 0)
    def _():
        acc[slot] += part

    @pl.when(f == nf - 1)
    def _():
        acc[slot] = acc[slot] * gate_ref[0]

        def issue(g, carry):
            r0 = pl.multiple_of(g * 8, 8)
            for k in range(8):
                pltpu.make_async_copy(acc.at[slot, pl.ds(r0 + k, 1), :],
                                      y_hbm.at[pl.ds(dst_ref[e * nrows + r0 + k], 1), :], ssem.at[slot]).start()
            return carry
        lax.fori_loop(0, nrows // 8, issue, 0)

    @pl.when((f == nf - 1) & (e == ne - 1))
    def _():
        if ne >= 2:
            scatter_wait(1 - slot)
        scatter_wait(slot)


def _moe(idx, dst, a, gate, w_gate, w_up, w_down, li):
    ne, nrows = idx.shape
    d = a.shape[1]
    ff = w_gate.shape[3]
    tf = 256
    nf = ff // tf
    grid_spec = pltpu.PrefetchScalarGridSpec(
        num_scalar_prefetch=2,
        grid=(ne, nf),
        in_specs=[pl.BlockSpec(memory_space=pl.ANY),
                  pl.BlockSpec((1, nrows, 1), lambda e, f, i_ref, d_ref: (e, 0, 0)),
                  pl.BlockSpec((1, 1, d, tf), lambda e, f, i_ref, d_ref: (li, e, 0, f)),
                  pl.BlockSpec((1, 1, d, tf), lambda e, f, i_ref, d_ref: (li, e, 0, f)),
                  pl.BlockSpec((1, 1, tf, d), lambda e, f, i_ref, d_ref: (li, e, f, 0))],
        out_specs=pl.BlockSpec(memory_space=pl.ANY),
        scratch_shapes=[pltpu.VMEM((nrows, d), F32), pltpu.VMEM((nrows, d), BF16),
                        pltpu.VMEM((2, nrows, d), F32),
                        pltpu.SemaphoreType.DMA((1,)), pltpu.SemaphoreType.DMA((2,))],
    )
    return pl.pallas_call(
        functools.partial(_moe_kernel, nrows=nrows, nf=nf),
        grid_spec=grid_spec,
        out_shape=jax.ShapeDtypeStruct((ne * nrows, d), F32),
        compiler_params=_cparams(("arbitrary", "arbitrary")),
        name="moe_experts",
    )(idx.reshape(-1), dst.reshape(-1), a, gate, w_gate, w_up, w_down)


COMBINE_TILE = 128
COMBINE_CHUNK = 256


def _combine_kernel(ts_ref, h_ref, y_hbm, st_ref, cn_ref, m_ref, mn_ref, g_ref, *rest, final, smax, tile_of):
    if final:
        o_ref, stage, acc, sem = rest
    else:
        h_out, a_out, stage, acc, sem = rest
    step = pl.program_id(0)
    nt = pl.num_programs(0)
    i = tile_of(step)
    tt = h_ref.shape[0]
    buf = step % 2

    def tile_range(t):
        return pl.multiple_of(ts_ref[0, t] & ~7, 8), ts_ref[1, t] + (ts_ref[0, t] & 7)

    def transfer(t, b, wait):
        s, n = tile_range(t)
        pieces = [(1 << k, (n & (1 << k)) != 0, pl.multiple_of(n & ~((2 << k) - 1), 8))
                  for k in range(int(math.log2(smax)), 2, -1)]
        pieces += [(1, j < (n & 7), (n & ~7) + j) for j in range(7)]
        for p, cond, off in pieces:
            @pl.when(cond)
            def _(p=p, off=off):
                src = s + off if p == 1 else pl.multiple_of(s + off, 8)
                cp = pltpu.make_async_copy(y_hbm.at[pl.ds(src, p), :], stage.at[b, pl.ds(off, p), :], sem.at[b])
                if wait:
                    cp.wait()
                else:
                    cp.start()

    @pl.when(step == 0)
    def _():
        stage[...] = jnp.zeros_like(stage)
        transfer(i, 0, wait=False)

    @pl.when(step + 1 < nt)
    def _():
        transfer(tile_of(step + 1), 1 - buf, wait=False)

    transfer(i, buf, wait=True)
    s, n = tile_range(i)
    acc[...] = jnp.zeros_like(acc)
    lo = st_ref[...] - s
    hi = lo + cn_ref[...]

    def body(c, carry):
        base = pl.multiple_of(c * COMBINE_CHUNK, COMBINE_CHUNK)
        jj = base + lax.broadcasted_iota(jnp.int32, (tt, COMBINE_CHUNK), 1)
        seg = jnp.where((jj >= lo) & (jj < hi), 1.0, 0.0).astype(BF16)
        acc[...] += _dot(seg, stage[buf, pl.ds(base, COMBINE_CHUNK), :].astype(BF16))
        return carry

    lax.fori_loop(0, (n + COMBINE_CHUNK - 1) // COMBINE_CHUNK, body, 0)
    hn = h_ref[...] + m_ref[0][5:6] * acc[...]
    if final:
        o_ref[...] = _rms(hn, g_ref[...])
    else:
        h_out[...] = hn
        mn = mn_ref[0]
        a_out[...] = (_rms(hn, g_ref[...]) * (1.0 + mn[1:2]) + mn[0:1]).astype(a_out.dtype)


def _combine(h, y, tile_start, start, count, mods, mods_next, g, tiles_per_sample, ctx_rows, final):
    r, d = h.shape
    tt = COMBINE_TILE
    smax = N_EXPERTS * tt
    per = tiles_per_sample * (ROW_TILE // tt)
    ctx_tiles = ctx_rows // tt
    lat = per - ctx_tiles
    const = lambda i, ts: (0, 0)

    if final:
        tile_of = lambda i: (i // lat) * per + ctx_tiles + i % lat
        steps = (r // tt // per) * lat
        mod_map = lambda i, ts: (i // lat, 0, 0)
        out_specs = pl.BlockSpec((tt, d), lambda i, ts: (i, 0))
        out_shape = jax.ShapeDtypeStruct((steps * tt, d), F32)
    else:
        tile_of = lambda i: i
        steps = r // tt
        mod_map = lambda i, ts: (jnp.where(i % per < ctx_tiles, 2, i // per), 0, 0)
        out_specs = [pl.BlockSpec((tt, d), lambda i, ts: (i, 0))] * 2
        out_shape = [jax.ShapeDtypeStruct((r, d), F32), jax.ShapeDtypeStruct((r, d), BF16)]
    row = lambda i, ts: (tile_of(i), 0)
    grid_spec = pltpu.PrefetchScalarGridSpec(
        num_scalar_prefetch=1,
        grid=(steps,),
        in_specs=[pl.BlockSpec((tt, d), row), pl.BlockSpec(memory_space=pl.ANY),
                  pl.BlockSpec((tt, 1), row), pl.BlockSpec((tt, 1), row),
                  pl.BlockSpec((1, 6, d), mod_map), pl.BlockSpec((1, 6, d), mod_map),
                  pl.BlockSpec((1, d), const)],
        out_specs=out_specs,
        scratch_shapes=[pltpu.VMEM((2, smax + COMBINE_CHUNK, d), F32), pltpu.VMEM((tt, d), F32),
                        pltpu.SemaphoreType.DMA((2,))],
    )
    return pl.pallas_call(
        functools.partial(_combine_kernel, final=final, smax=smax, tile_of=tile_of),
        grid_spec=grid_spec,
        out_shape=out_shape,
        compiler_params=_cparams(("arbitrary",)),
        name="moe_combine",
    )(tile_start, h, y, start, count, mods, mods_next, g.reshape(1, d))


def _route(aff, b, l, tc, with_ctx):
    aff3 = aff.reshape(b, l, N_EXPERTS)
    ne = N_EXPERTS
    ids, gates = [], []
    sel = jnp.zeros((b, l, ne), jnp.bool_)
    segs = [(tc, l - tc)] + ([(0, tc)] if with_ctx else [])
    for start, n in segs:
        cap = EC_CAPACITY * n // ne
        seg_aff = jnp.swapaxes(aff3[:, start:start + n], 1, 2)
        _, idx = lax.top_k(seg_aff, cap)
        idx = jnp.sort(idx, axis=-1)
        g = jnp.take_along_axis(seg_aff, idx, axis=-1)
        hit = jnp.any(idx[..., None] == jnp.arange(n, dtype=idx.dtype), axis=2)
        sel = sel.at[:, start:start + n].set(jnp.swapaxes(hit, 1, 2))
        rows = idx + start + (jnp.arange(b, dtype=jnp.int32) * l)[:, None, None]
        ids.append(jnp.swapaxes(rows, 0, 1).reshape(ne, b * cap))
        gates.append(jnp.swapaxes(g, 0, 1).reshape(ne, b * cap))
    idx_all = jnp.concatenate(ids, 1).astype(jnp.int32)
    gate_all = jnp.concatenate(gates, 1)[..., None]
    sel_i = sel.reshape(b * l, ne).astype(jnp.int32)
    count = jnp.sum(sel_i, axis=1)
    start_row = jnp.cumsum(count) - count
    slot = start_row[:, None] + jnp.cumsum(sel_i, axis=1) - sel_i
    dst_all = jnp.take_along_axis(slot.T, idx_all, axis=1).astype(jnp.int32)
    tile_n = jnp.sum(count.reshape(-1, COMBINE_TILE), axis=1)
    tile_info = jnp.stack([start_row[::COMBINE_TILE], tile_n]).astype(jnp.int32)
    return idx_all, gate_all, dst_all, tile_info, start_row[:, None].astype(jnp.int32), \
        count[:, None].astype(jnp.int32)


def kernel(x, c, ctx, c_ctx, w_mod, b_mod, norm1_g, norm2_g, w_in, gdn_conv_w, gdn_a_log, gdn_dt_bias,
           gdn_norm_g, sc_conv_w, diff_lambda, diff_norm_g, w_out, w_router, w_e_gate, w_e_up, w_e_down,
           final_norm_g):
    b, t, d = x.shape
    tc = ctx.shape[1]
    l = tc + t
    r = b * l
    depth = w_mod.shape[0]
    tiles = l // ROW_TILE
    nh = GDN_HEADS

    cond = jnp.concatenate([c, c_ctx[None, :], jnp.zeros((8 - b - 1, d), F32)], 0)
    mods_all = _modulation(cond, w_mod, b_mod)[:, :3].reshape(depth, 3, 6, d)
    cos_t, sin_t = _rope_tables(t, tc)

    h = jnp.concatenate([ctx, x], 1).reshape(r, d)
    a = _norm_mod(h, norm1_g[0], mods_all[0], tiles)
    out = None
    for li in range(depth):
        with_ctx = li < depth - 1
        lam_init = 0.8 - 0.6 * math.exp(-0.3 * li)
        wl = w_in[li]
        a_in = 4 * GDN_W
        w_main = jnp.concatenate([wl[:, :a_in], wl[:, a_in + 4 * nh:]], 1).astype(BF16)
        w_ba = jnp.concatenate([wl[:, a_in:a_in + 4 * nh], jnp.zeros((d, LANE - 4 * nh), F32)], 1).astype(BF16)
        u = _matmul(a, w_main, BF16, r // 8, GDN_W)
        ba = _matmul(a, w_ba, F32, r // 8, LANE)
        u3 = u.reshape(b, l, N_MAIN)

        qkv = _gdn_prep(u3, gdn_conv_w[li], tc)
        prm = jnp.zeros((8, LANE), F32)
        prm = prm.at[0, 2 * nh:4 * nh].set(-jnp.exp(gdn_a_log[li].reshape(-1)))
        prm = prm.at[1, 2 * nh:4 * nh].set(gdn_dt_bias[li].reshape(-1))
        o_f, o_b = _gdn(qkv, ba.reshape(b, l, LANE), prm, tc, GDN_CHUNK)
        y_b = _short_conv(u3, sc_conv_w[li], tc)
        q_t, k_r, v_t = _attn_prep(u3, cos_t, sin_t)
        y_c = _diff_attn(q_t, k_r, v_t, diff_lambda[li], diff_norm_g[li], lam_init, tc)

        h, a2, aff = _outproj(o_f.reshape(r, GDN_W), o_b.reshape(r, GDN_W), u, y_b.reshape(r, SC_W),
                              y_c.reshape(r, DIFF_W), w_out[li].astype(BF16), h, mods_all[li],
                              gdn_norm_g[li], norm2_g[li], w_router[li], tiles)
        idx, gate, dst, tile_start, start_row, count = _route(aff, b, l, tc, with_ctx)
        y = _moe(idx, dst, a2, gate, w_e_gate, w_e_up, w_e_down, li)
        if with_ctx:
            h, a = _combine(h, y, tile_start, start_row, count, mods_all[li], mods_all[li + 1],
                            norm1_g[li + 1], tiles, tc, final=False)
        else:
            out = _combine(h, y, tile_start, start_row, count, mods_all[li], mods_all[li],
                           final_norm_g, tiles, tc, final=True)
    return out.reshape(b, t, d)
```

```python
import functools
import math

import numpy as np
import jax
import jax.numpy as jnp
from jax import lax
from jax.experimental import pallas as pl
from jax.experimental.pallas import tpu as pltpu

F32 = jnp.float32
BF16 = jnp.bfloat16

EPS = 1e-6
GRID_W = 64
GDN_HEADS = 6
GDN_D = 128
GDN_W = GDN_HEADS * GDN_D
SC_W = 512
DIFF_HEADS = 6
DIFF_DH = 64
DIFF_DV = 128
DIFF_W = DIFF_HEADS * DIFF_DV
ROPE_BASE = 10000.0
ROPE_PAIRS = DIFF_DH // 4
N_EXPERTS = 16
EC_CAPACITY = 2

COL_Z = 3 * GDN_W
COL_SC = COL_Z + GDN_W
COL_ATT = COL_SC + 3 * SC_W
N_MAIN = COL_ATT + 3 * DIFF_W
LANE = 128
ROW_TILE = 256
GDN_CHUNK = 128
VT_ROWS = DIFF_DV + 16
VMEM_LIMIT = 56 * 1024 * 1024


def _cparams(sem, vmem=VMEM_LIMIT):
    return pltpu.CompilerParams(dimension_semantics=sem, vmem_limit_bytes=vmem)


def _sigmoid(x):
    return 1.0 / (1.0 + jnp.exp(-x))


def _silu(x):
    return x * _sigmoid(x)


def _softplus(x):
    return jnp.maximum(x, 0.0) + jnp.log(1.0 + jnp.exp(-jnp.abs(x)))


def _rms(x, g):
    return x * lax.rsqrt(jnp.mean(x * x, axis=-1, keepdims=True) + EPS) * g


def _dot(a, b, **kw):
    return jnp.dot(a, b, preferred_element_type=F32, **kw)


def _dot_nt(a, b):
    return lax.dot_general(a, b, (((1,), (1,)), ((), ())), preferred_element_type=F32)


def _mod_kernel(s_ref, w_ref, b_ref, o_ref):
    k = pl.program_id(1)
    tk = w_ref.shape[1]

    @pl.when(k == 0)
    def _():
        o_ref[0] = jnp.broadcast_to(b_ref[0], o_ref.shape[1:])

    s = _silu(s_ref[:, pl.ds(pl.multiple_of(k * tk, tk), tk)]).astype(BF16)
    o_ref[0] += _dot(s, w_ref[0].astype(BF16))


def _modulation(cond, w_mod, b_mod):
    depth, d, n = w_mod.shape
    tk = 256
    return pl.pallas_call(
        _mod_kernel,
        grid=(depth, d // tk),
        in_specs=[pl.BlockSpec((8, d), lambda l, k: (0, 0)),
                  pl.BlockSpec((1, tk, n), lambda l, k: (l, k, 0)),
                  pl.BlockSpec((1, 1, n), lambda l, k: (l, 0, 0))],
        out_specs=pl.BlockSpec((1, 8, n), lambda l, k: (l, 0, 0)),
        out_shape=jax.ShapeDtypeStruct((depth, 8, n), F32),
        compiler_params=_cparams(("arbitrary", "arbitrary")),
        name="modulation",
    )(cond, w_mod, b_mod.reshape(depth, 1, n))


def _mod_row_map(tiles_per_sample):
    def index_map(i):
        return (jnp.where(i % tiles_per_sample == 0, 2, i // tiles_per_sample), 0, 0)
    return index_map


def _norm_mod_kernel(h_ref, g_ref, m_ref, a_ref):
    m = m_ref[0]
    a = _rms(h_ref[...], g_ref[...]) * (1.0 + m[1:2]) + m[0:1]
    a_ref[...] = a.astype(a_ref.dtype)


def _norm_mod(h, g, mods, tiles_per_sample):
    r, d = h.shape
    return pl.pallas_call(
        _norm_mod_kernel,
        grid=(r // ROW_TILE,),
        in_specs=[pl.BlockSpec((ROW_TILE, d), lambda i: (i, 0)),
                  pl.BlockSpec((1, d), lambda i: (0, 0)),
                  pl.BlockSpec((1, 6, d), _mod_row_map(tiles_per_sample))],
        out_specs=pl.BlockSpec((ROW_TILE, d), lambda i: (i, 0)),
        out_shape=jax.ShapeDtypeStruct((r, d), BF16),
        compiler_params=_cparams(("arbitrary",)),
        name="norm_mod",
    )(h, g.reshape(1, d), mods)


def _mm_kernel(a_ref, w_ref, o_ref):
    o_ref[...] = _dot(a_ref[...], w_ref[0]).astype(o_ref.dtype)


def _matmul(a, w, li, out_dtype, tm, tn):
    r, k = a.shape
    n = w.shape[2]
    return pl.pallas_call(
        _mm_kernel,
        grid=(n // tn, r // tm),
        in_specs=[pl.BlockSpec((tm, k), lambda j, i: (i, 0)),
                  pl.BlockSpec((1, k, tn), lambda j, i: (li, 0, j))],
        out_specs=pl.BlockSpec((tm, tn), lambda j, i: (i, j)),
        out_shape=jax.ShapeDtypeStruct((r, n), out_dtype),
        compiler_params=_cparams(("arbitrary", "arbitrary")),
        name="in_proj",
    )(a, w)


def _conv3(x, w, tc):
    l = x.shape[0]
    row = lax.broadcasted_iota(jnp.int32, x.shape, 0)
    prev = jnp.where((row == 0) | (row == tc), 0.0, pltpu.roll(x, 1, 0))
    nxt = jnp.where((row == tc - 1) | (row == l - 1), 0.0, pltpu.roll(x, l - 1, 0))
    return prev * w[0:1] + x * w[1:2] + nxt * w[2:3]


def _gdn_prep_kernel(u_ref, w_ref, o_ref, *, tc):
    j = pl.program_id(1)
    y = _silu(_conv3(u_ref[0].astype(F32), w_ref[...], tc))
    inv = lax.rsqrt(jnp.sum(y * y, axis=-1, keepdims=True) + EPS)
    nh = GDN_HEADS
    fac = jnp.where(j < nh, inv * GDN_D ** -0.5, jnp.where(j < 2 * nh, inv, 1.0))
    o_ref[0] = (y * fac).astype(o_ref.dtype)


def _gdn_prep(u3, conv_w, tc):
    b, l, _ = u3.shape
    nblk = 3 * GDN_W // LANE
    return pl.pallas_call(
        functools.partial(_gdn_prep_kernel, tc=tc),
        grid=(b, nblk),
        in_specs=[pl.BlockSpec((1, l, LANE), lambda bi, j: (bi, 0, j)),
                  pl.BlockSpec((3, LANE), lambda bi, j: (0, j))],
        out_specs=pl.BlockSpec((1, l, LANE), lambda bi, j: (bi, 0, j)),
        out_shape=jax.ShapeDtypeStruct((b, l, 3 * GDN_W), BF16),
        compiler_params=_cparams(("arbitrary", "arbitrary")),
        name="gdn_prep",
    )(u3, conv_w)


def _sconv_kernel(b_ref, c_ref, x_ref, w_ref, o_ref, *, tc):
    inner = c_ref[0].astype(F32) * x_ref[0].astype(F32)
    o_ref[0] = (b_ref[0].astype(F32) * _conv3(inner, w_ref[...], tc)).astype(o_ref.dtype)


def _short_conv(u3, conv_w, tc):
    b, l, _ = u3.shape
    nblk = SC_W // LANE
    base = COL_SC // LANE

    def spec(k):
        return pl.BlockSpec((1, l, LANE), lambda bi, j: (bi, 0, base + k * nblk + j))

    return pl.pallas_call(
        functools.partial(_sconv_kernel, tc=tc),
        grid=(b, nblk),
        in_specs=[spec(0), spec(1), spec(2), pl.BlockSpec((3, LANE), lambda bi, j: (0, j))],
        out_specs=pl.BlockSpec((1, l, LANE), lambda bi, j: (bi, 0, j)),
        out_shape=jax.ShapeDtypeStruct((b, l, SC_W), BF16),
        compiler_params=_cparams(("arbitrary", "arbitrary")),
        name="short_conv",
    )(u3, u3, u3, conv_w)


def _rope(x, cos, sin):
    lane = lax.broadcasted_iota(jnp.int32, x.shape, 1)
    swapped = jnp.where(lane % 32 < 16, pltpu.roll(x, LANE - 16, 1), pltpu.roll(x, 16, 1))
    return x * cos + swapped * sin


def _attn_prep_kernel(q_ref, k_ref, v_ref, cos_ref, sin_ref, qt_ref, ko_ref, vt_ref):
    cos = cos_ref[...]
    sin = sin_ref[...]
    q = _rope(q_ref[0].astype(F32), cos, sin) * DIFF_DH ** -0.5
    qt_ref[0, 0] = q.T.astype(qt_ref.dtype)
    ko_ref[0] = _rope(k_ref[0].astype(F32), cos, sin).astype(ko_ref.dtype)
    vt_ref[0, 0, 0:DIFF_DV, :] = v_ref[0].astype(F32).T.astype(vt_ref.dtype)
    vt_ref[0, 0, DIFF_DV:, :] = jnp.ones((VT_ROWS - DIFF_DV, v_ref.shape[1]), vt_ref.dtype)


def _attn_prep(u3, cos_t, sin_t):
    b, l, _ = u3.shape
    nh = DIFF_HEADS
    base = COL_ATT // LANE
    tbl = pl.BlockSpec((l, LANE), lambda bi, j: (0, 0))
    tspec = pl.BlockSpec((1, 1, LANE, l), lambda bi, j: (bi, j, 0, 0))
    tshape = jax.ShapeDtypeStruct((b, nh, LANE, l), BF16)
    vspec = pl.BlockSpec((1, 1, VT_ROWS, l), lambda bi, j: (bi, j, 0, 0))
    vshape = jax.ShapeDtypeStruct((b, nh, VT_ROWS, l), BF16)
    return pl.pallas_call(
        _attn_prep_kernel,
        grid=(b, nh),
        in_specs=[pl.BlockSpec((1, l, LANE), lambda bi, j: (bi, 0, base + j)),
                  pl.BlockSpec((1, l, LANE), lambda bi, j: (bi, 0, base + nh + j)),
                  pl.BlockSpec((1, l, LANE), lambda bi, j: (bi, 0, base + 2 * nh + j)),
                  tbl, tbl],
        out_specs=[tspec, pl.BlockSpec((1, l, LANE), lambda bi, j: (bi, 0, j)), vspec],
        out_shape=[tshape, jax.ShapeDtypeStruct((b, l, DIFF_W), BF16), vshape],
        compiler_params=_cparams(("arbitrary", "arbitrary")),
        name="attn_prep",
    )(u3, u3, u3, cos_t, sin_t)


def _rope_tables(t, tc):
    rows = t // GRID_W
    row = np.repeat(np.arange(rows, dtype=np.float32), GRID_W)
    col = np.tile(np.arange(GRID_W, dtype=np.float32), rows)
    inv = (np.float32(ROPE_BASE) ** (-np.arange(ROPE_PAIRS, dtype=np.float32) / np.float32(ROPE_PAIRS))).astype(np.float32)
    lane = np.arange(LANE)
    axis = (lane // 32) % 2
    pos = np.where(axis[None, :] == 0, row[:, None], col[:, None]).astype(np.float32)
    ang = (pos * inv[lane % 16][None, :]).astype(np.float32)
    sign = np.where(lane % 32 < 16, -1.0, 1.0)[None, :].astype(np.float32)
    cos_t = np.concatenate([np.ones((tc, LANE), np.float32), np.cos(ang).astype(np.float32)], 0)
    sin_t = np.concatenate([np.zeros((tc, LANE), np.float32), (np.sin(ang) * sign).astype(np.float32)], 0)
    return jnp.asarray(cos_t), jnp.asarray(sin_t)


def _gdn_kernel(qf_ref, kf_ref, vf_ref, gf_ref, qb_ref, kb_ref, vb_ref, gb_ref, prm_ref,
                of_ref, ob_ref, s_ref, *, chunk):
    c = chunk
    nh = GDN_HEADS

    @pl.when(pl.program_id(1) == 0)
    def _():
        s_ref[...] = jnp.zeros_like(s_ref)

    ii = lax.broadcasted_iota(jnp.int32, (c, c), 0)
    jj = lax.broadcasted_iota(jnp.int32, (c, c), 1)
    neg_a = prm_ref[0:1, :]
    dt_b = prm_ref[1:2, :]
    blk = [(ii >> sh) == (jj >> sh) for sh in range(3, int(math.log2(c)) + 1)]
    dirs = ((qf_ref, kf_ref, vf_ref, gf_ref, of_ref), (qb_ref, kb_ref, vb_ref, gb_ref, ob_ref))
    offm = [blk[lvl] & ~blk[lvl - 1] for lvl in range(1, len(blk))]

    units = []
    for d, (q_ref, k_ref, v_ref, g_ref, o_ref) in enumerate(dirs):
        incl = (ii >= jj) if d == 0 else (ii <= jj)
        strict = (ii > jj) if d == 0 else (ii < jj)
        raw = g_ref[0]
        g_all = neg_a * _softplus(raw + dt_b)
        beta_all = _sigmoid(raw)
        tri = incl.astype(F32)
        cum = _dot(tri, g_all, precision=lax.Precision.HIGHEST)
        cum_t = _dot(g_all.T, tri.T, precision=lax.Precision.HIGHEST)
        last = c - 1 if d == 0 else 0
        for h in range(nh):
            cb = d * nh + h
            cg = 2 * nh + cb
            sl = slice(h * GDN_D, (h + 1) * GDN_D)
            cc = cum[:, cg:cg + 1]
            units.append(dict(
                cb=cb, sl=sl, o_ref=o_ref, incl=incl, strict=strict, cc=cc,
                beta=beta_all[:, cb:cb + 1], cr=cum_t[cg:cg + 1, :], tot=cum[last:last + 1, cg:cg + 1],
                q=q_ref[0, :, sl], k=k_ref[0, :, sl], v=v_ref[0, :, sl]))

    for un in units:
        un["kbeta"] = un["k"].astype(F32) * un["beta"]
        un["gram"] = _dot_nt(jnp.concatenate([un["kbeta"].astype(BF16), un["q"]], axis=0), un["k"])
    for un in units:
        decay = jnp.exp(jnp.where(un["incl"], un["cc"] - un["cr"], -1e30))
        un["low"] = jnp.where(un["strict"], un["gram"][:c] * decay, 0.0)
        un["attn"] = (un["gram"][c:] * decay).astype(BF16)
        un["p"] = jnp.where(blk[0], -un["low"], 0.0)
        un["n"] = un["p"]
    for _ in range(2):
        for un in units:
            pb = un["p"].astype(BF16)
            un["p"] = _dot(pb, pb)
        for un in units:
            un["n"] = un["n"] + un["p"] + _dot(un["n"].astype(BF16), un["p"].astype(BF16))
    for om in offm:
        for un in units:
            off = jnp.where(om, un["low"], 0.0).astype(BF16)
            un["m1"] = off + _dot(off, un["n"].astype(BF16))
        for un in units:
            un["n"] = un["n"] - un["m1"] - _dot(un["n"].astype(BF16), un["m1"].astype(BF16))
    for un in units:
        eg = jnp.exp(un["cc"])
        rhs = jnp.concatenate([un["v"].astype(F32) * un["beta"], un["kbeta"] * eg], axis=1)
        sol = rhs + _dot(un["n"].astype(BF16), rhs.astype(BF16))
        un["u"] = sol[:, :GDN_D]
        qd = un["q"].astype(F32) * eg
        un["wq"] = jnp.concatenate([sol[:, GDN_D:].astype(BF16), qd.astype(BF16)], axis=0)
        un["kdt"] = (un["k"].astype(F32) * jnp.exp(un["tot"] - un["cc"])).T.astype(BF16)
    for un in units:
        un["s"] = s_ref[un["cb"]]
        un["ws"] = _dot(un["wq"], un["s"].astype(BF16))
    for un in units:
        un["v_new"] = (un["u"] - un["ws"][:c]).astype(BF16)
        un["o_ref"][0, :, un["sl"]] = un["ws"][c:] + _dot(un["attn"], un["v_new"])
    for un in units:
        s_ref[un["cb"]] = un["s"] * jnp.exp(un["tot"]) + _dot(un["kdt"], un["v_new"])


def _gdn(qkv, ba, prm, tc, chunk):
    b, l, _ = qkv.shape
    nc = l // chunk
    ncc = tc // chunk

    def fwd(col):
        return lambda bi, n: (bi, n, col)

    def bwd(col):
        return lambda bi, n: (bi, jnp.where(n < ncc, ncc - 1 - n, nc - 1 + ncc - n), col)

    blk = (1, chunk, GDN_W)
    gblk = (1, chunk, LANE)
    out = jax.ShapeDtypeStruct((b, l, GDN_W), F32)
    return pl.pallas_call(
        functools.partial(_gdn_kernel, chunk=chunk),
        grid=(b, nc),
        in_specs=[pl.BlockSpec(blk, fwd(0)), pl.BlockSpec(blk, fwd(1)), pl.BlockSpec(blk, fwd(2)),
                  pl.BlockSpec(gblk, fwd(0)),
                  pl.BlockSpec(blk, bwd(0)), pl.BlockSpec(blk, bwd(1)), pl.BlockSpec(blk, bwd(2)),
                  pl.BlockSpec(gblk, bwd(0)),
                  pl.BlockSpec((8, LANE), lambda bi, n: (0, 0))],
        out_specs=[pl.BlockSpec(blk, fwd(0)), pl.BlockSpec(blk, bwd(0))],
        out_shape=[out, out],
        scratch_shapes=[pltpu.VMEM((2 * GDN_HEADS, GDN_D, GDN_D), F32)],
        compiler_params=_cparams(("arbitrary", "arbitrary")),
        name="gdn",
    )(qkv, qkv, qkv, ba, qkv, qkv, qkv, ba, prm)


def _attn_kernel(qt_ref, k_ref, vt_ref, lp_ref, g_ref, o_ref, s_sc, acc_sc, *, tq, tk, nk_ctx, nk_all, lam_init):
    qt = qt_ref[0, 0]
    sub = lax.broadcasted_iota(jnp.int32, qt.shape, 0)
    zero = jnp.zeros_like(qt)
    qq = jnp.concatenate([jnp.where(sub < DIFF_DH, qt, zero), jnp.where(sub >= DIFF_DH, qt, zero)], axis=1)
    lp = lp_ref[...]
    lam = (jnp.exp(jnp.sum(lp[0:1] * lp[1:2], axis=1, keepdims=True))
           - jnp.exp(jnp.sum(lp[2:3] * lp[3:4], axis=1, keepdims=True)) + lam_init)

    def run(nkv):
        n = nkv * tk
        s_sc[0:n, :] = _dot(k_ref[0, 0:n, :], qq)
        m = jnp.full((1, 2 * tq), -jnp.inf, F32)
        for c in range(nkv):
            s = s_sc[c * tk:(c + 1) * tk, :]
            m_new = jnp.maximum(m, jnp.max(s, axis=0, keepdims=True))
            p = jnp.exp(s - m_new).astype(BF16)
            pv = _dot(vt_ref[0, 0, :, c * tk:(c + 1) * tk], p)
            acc_sc[...] = pv if c == 0 else jnp.exp(m - m_new) * acc_sc[...] + pv
            m = m_new
        on = acc_sc[0:DIFF_DV, :] / acc_sc[DIFF_DV:DIFF_DV + 1, :]
        o = on[:, :tq] - lam * on[:, tq:]
        y = o * lax.rsqrt(jnp.mean(o * o, axis=0, keepdims=True) + EPS) * g_ref[...] * (1.0 - lam_init)
        o_ref[0] = y.T.astype(o_ref.dtype)

    is_ctx = pl.program_id(2) == 0

    @pl.when(is_ctx)
    def _():
        run(nk_ctx)

    @pl.when(jnp.logical_not(is_ctx))
    def _():
        run(nk_all)


def _diff_attn(qt, k, vt, lam_params, norm_g, lam_init, tc):
    b, l, _ = k.shape
    tq = ROW_TILE
    tk = 256
    kern = functools.partial(_attn_kernel, tq=tq, tk=tk, nk_ctx=tc // tk, nk_all=l // tk, lam_init=lam_init)
    return pl.pallas_call(
        kern,
        grid=(b, DIFF_HEADS, l // tq),
        in_specs=[pl.BlockSpec((1, 1, LANE, tq), lambda bi, h, i: (bi, h, 0, i)),
                  pl.BlockSpec((1, l, LANE), lambda bi, h, i: (bi, 0, h)),
                  pl.BlockSpec((1, 1, VT_ROWS, l), lambda bi, h, i: (bi, h, 0, 0)),
                  pl.BlockSpec((4, DIFF_DH), lambda bi, h, i: (0, 0)),
                  pl.BlockSpec((DIFF_DV, 1), lambda bi, h, i: (0, 0))],
        out_specs=pl.BlockSpec((1, tq, LANE), lambda bi, h, i: (bi, i, h)),
        out_shape=jax.ShapeDtypeStruct((b, l, DIFF_W), BF16),
        scratch_shapes=[pltpu.VMEM((l, 2 * tq), F32), pltpu.VMEM((VT_ROWS, 2 * tq), F32)],
        compiler_params=_cparams(("arbitrary", "arbitrary", "arbitrary")),
        name="diff_attn",
    )(qt, k, vt, lam_params, norm_g.reshape(DIFF_DV, 1))


def _outproj_kernel(of_ref, ob_ref, z_ref, yb_ref, yc_ref, w_ref, h_ref, m_ref, gn_ref, g2_ref, wr_ref,
                    h_out, a_out, aff_out):
    o = of_ref[...] + ob_ref[...]
    gn = gn_ref[...]
    parts = []
    for hd in range(GDN_HEADS):
        sl = slice(hd * GDN_D, (hd + 1) * GDN_D)
        parts.append((_rms(o[:, sl], gn) * _silu(z_ref[:, sl].astype(F32))).astype(BF16))
    ya = jnp.concatenate(parts, axis=1)
    acc = _dot(ya, w_ref[0, 0:GDN_W, :])
    acc += _dot(yb_ref[...], w_ref[0, GDN_W:GDN_W + SC_W, :])
    acc += _dot(yc_ref[...], w_ref[0, GDN_W + SC_W:, :])
    m = m_ref[0]
    hn = h_ref[...] + m[2:3] * acc
    h_out[...] = hn
    a = _rms(hn, g2_ref[...]) * (1.0 + m[4:5]) + m[3:4]
    a_out[...] = a
    a_hi = a.astype(BF16)
    a_lo = (a - a_hi.astype(F32)).astype(BF16)
    wr = wr_ref[...]
    w_hi = wr.astype(BF16)
    w_lo = (wr - w_hi.astype(F32)).astype(BF16)
    prod = _dot(jnp.concatenate([a_hi, a_lo], axis=0), jnp.concatenate([w_hi, w_lo], axis=1))
    tm, ne = a.shape[0], wr.shape[1]
    logits = (prod[:tm, :ne] + prod[:tm, ne:]) + (prod[tm:, :ne] + prod[tm:, ne:])
    e = jnp.exp(logits - jnp.max(logits, axis=-1, keepdims=True))
    aff_out[...] = e / jnp.sum(e, axis=-1, keepdims=True)


def _outproj(o_f, o_b, u, y_b, y_c, w_out, li, h, mods, gdn_g, g2, w_router, tiles_per_sample):
    r, d = h.shape
    tm = ROW_TILE
    zb = COL_Z // GDN_W
    row = lambda i: (i, 0)
    const = lambda i: (0, 0)
    return pl.pallas_call(
        _outproj_kernel,
        grid=(r // tm,),
        in_specs=[pl.BlockSpec((tm, GDN_W), row), pl.BlockSpec((tm, GDN_W), row),
                  pl.BlockSpec((tm, GDN_W), lambda i: (i, zb)),
                  pl.BlockSpec((tm, SC_W), row), pl.BlockSpec((tm, DIFF_W), row),
                  pl.BlockSpec((1, d, d), lambda i: (li, 0, 0)), pl.BlockSpec((tm, d), row),
                  pl.BlockSpec((1, 6, d), _mod_row_map(tiles_per_sample)),
                  pl.BlockSpec((1, GDN_D), const), pl.BlockSpec((1, d), const),
                  pl.BlockSpec((d, N_EXPERTS), const)],
        out_specs=[pl.BlockSpec((tm, d), row), pl.BlockSpec((tm, d), row),
                   pl.BlockSpec((tm, N_EXPERTS), row)],
        out_shape=[jax.ShapeDtypeStruct((r, d), F32), jax.ShapeDtypeStruct((r, d), F32),
                   jax.ShapeDtypeStruct((r, N_EXPERTS), F32)],
        compiler_params=_cparams(("arbitrary",)),
        name="out_proj",
    )(o_f, o_b, u, y_b, y_c, w_out, h, mods, gdn_g.reshape(1, GDN_D), g2.reshape(1, d), w_router)


def _moe_kernel(idx_ref, dst_ref, a_hbm, gate_ref, wg_ref, wu_ref, wd_ref, y_hbm, x32, xb, acc, gsem, ssem,
                *, nrows, nf):
    e = pl.program_id(0)
    f = pl.program_id(1)
    ne = pl.num_programs(0)
    slot = e % 2

    def gather(expert):
        def issue(g, carry):
            r0 = pl.multiple_of(g * 8, 8)
            for k in range(8):
                pltpu.make_async_copy(a_hbm.at[pl.ds(idx_ref[expert * nrows + r0 + k], 1), :],
                                      x32.at[pl.ds(r0 + k, 1), :], gsem.at[0]).start()
            return carry
        lax.fori_loop(0, nrows // 8, issue, 0)

    def scatter_wait(s):
        pltpu.make_async_copy(acc.at[s], y_hbm.at[pl.ds(0, nrows), :], ssem.at[s]).wait()

    @pl.when((e == 0) & (f == 0))
    def _():
        gather(0)

    @pl.when(f == 0)
    def _():
        pltpu.make_async_copy(a_hbm.at[pl.ds(0, nrows), :], x32, gsem.at[0]).wait()
        xb[...] = x32[...].astype(BF16)

    @pl.when((f == 0) & (e + 1 < ne))
    def _():
        gather(e + 1)

    @pl.when((f == 0) & (e >= 2))
    def _():
        scatter_wait(slot)

    x = xb[...]
    hid = _silu(_dot(x, wg_ref[0, 0].astype(BF16))) * _dot(x, wu_ref[0, 0].astype(BF16))
    part = _dot(hid.astype(BF16), wd_ref[0, 0].astype(BF16))

    @pl.when(f == 0)
    def _():
        acc[slot] = part

    @pl.when(f > 0)
    def _():
        acc[slot] += part

    @pl.when(f == nf - 1)
    def _():
        acc[slot] = acc[slot] * gate_ref[0]

        def issue(g, carry):
            r0 = pl.multiple_of(g * 8, 8)
            for k in range(8):
                pltpu.make_async_copy(acc.at[slot, pl.ds(r0 + k, 1), :],
                                      y_hbm.at[pl.ds(dst_ref[e * nrows + r0 + k], 1), :], ssem.at[slot]).start()
            return carry
        lax.fori_loop(0, nrows // 8, issue, 0)

    @pl.when((f == nf - 1) & (e == ne - 1))
    def _():
        if ne >= 2:
            scatter_wait(1 - slot)
        scatter_wait(slot)


def _moe(idx, dst, a, gate, w_gate, w_up, w_down, li):
    ne, nrows = idx.shape
    d = a.shape[1]
    ff = w_gate.shape[3]
    tf = 256
    nf = ff // tf
    grid_spec = pltpu.PrefetchScalarGridSpec(
        num_scalar_prefetch=2,
        grid=(ne, nf),
        in_specs=[pl.BlockSpec(memory_space=pl.ANY),
                  pl.BlockSpec((1, nrows, 1), lambda e, f, i_ref, d_ref: (e, 0, 0)),
                  pl.BlockSpec((1, 1, d, tf), lambda e, f, i_ref, d_ref: (li, e, 0, f)),
                  pl.BlockSpec((1, 1, d, tf), lambda e, f, i_ref, d_ref: (li, e, 0, f)),
                  pl.BlockSpec((1, 1, tf, d), lambda e, f, i_ref, d_ref: (li, e, f, 0))],
        out_specs=pl.BlockSpec(memory_space=pl.ANY),
        scratch_shapes=[pltpu.VMEM((nrows, d), F32), pltpu.VMEM((nrows, d), BF16),
                        pltpu.VMEM((2, nrows, d), F32),
                        pltpu.SemaphoreType.DMA((1,)), pltpu.SemaphoreType.DMA((2,))],
    )
    return pl.pallas_call(
        functools.partial(_moe_kernel, nrows=nrows, nf=nf),
        grid_spec=grid_spec,
        out_shape=jax.ShapeDtypeStruct((ne * nrows, d), F32),
        compiler_params=_cparams(("arbitrary", "arbitrary")),
        name="moe_experts",
    )(idx.reshape(-1), dst.reshape(-1), a, gate, w_gate, w_up, w_down)


COMBINE_TILE = 128
COMBINE_CHUNK = 256


def _combine_kernel(ts_ref, h_ref, y_hbm, st_ref, cn_ref, m_ref, mn_ref, g_ref, *rest, final, smax, tile_of):
    if final:
        o_ref, stage, acc, sem = rest
    else:
        h_out, a_out, stage, acc, sem = rest
    step = pl.program_id(0)
    nt = pl.num_programs(0)
    i = tile_of(step)
    tt = h_ref.shape[0]
    buf = step % 2

    def tile_range(t):
        return pl.multiple_of(ts_ref[0, t] & ~7, 8), ts_ref[1, t] + (ts_ref[0, t] & 7)

    def transfer(t, b, wait):
        s, n = tile_range(t)
        pieces = [(1 << k, (n & (1 << k)) != 0, pl.multiple_of(n & ~((2 << k) - 1), 8))
                  for k in range(int(math.log2(smax)), 2, -1)]
        pieces += [(1, j < (n & 7), (n & ~7) + j) for j in range(7)]
        for p, cond, off in pieces:
            @pl.when(cond)
            def _(p=p, off=off):
                src = s + off if p == 1 else pl.multiple_of(s + off, 8)
                cp = pltpu.make_async_copy(y_hbm.at[pl.ds(src, p), :], stage.at[b, pl.ds(off, p), :], sem.at[b])
                if wait:
                    cp.wait()
                else:
                    cp.start()

    @pl.when(step == 0)
    def _():
        stage[...] = jnp.zeros_like(stage)
        transfer(i, 0, wait=False)

    @pl.when(step + 1 < nt)
    def _():
        transfer(tile_of(step + 1), 1 - buf, wait=False)

    transfer(i, buf, wait=True)
    s, n = tile_range(i)
    acc[...] = jnp.zeros_like(acc)
    lo = st_ref[...] - s
    hi = lo + cn_ref[...]

    def body(c, carry):
        base = pl.multiple_of(c * COMBINE_CHUNK, COMBINE_CHUNK)
        jj = base + lax.broadcasted_iota(jnp.int32, (tt, COMBINE_CHUNK), 1)
        seg = jnp.where((jj >= lo) & (jj < hi), 1.0, 0.0).astype(BF16)
        acc[...] += _dot(seg, stage[buf, pl.ds(base, COMBINE_CHUNK), :].astype(BF16))
        return carry

    lax.fori_loop(0, (n + COMBINE_CHUNK - 1) // COMBINE_CHUNK, body, 0)
    hn = h_ref[...] + m_ref[0][5:6] * acc[...]
    if final:
        o_ref[...] = _rms(hn, g_ref[...])
    else:
        h_out[...] = hn
        mn = mn_ref[0]
        a_out[...] = (_rms(hn, g_ref[...]) * (1.0 + mn[1:2]) + mn[0:1]).astype(a_out.dtype)


def _combine(h, y, tile_start, start, count, mods, mods_next, g, tiles_per_sample, ctx_rows, final):
    r, d = h.shape
    tt = COMBINE_TILE
    smax = N_EXPERTS * tt
    per = tiles_per_sample * (ROW_TILE // tt)
    ctx_tiles = ctx_rows // tt
    lat = per - ctx_tiles
    const = lambda i, ts: (0, 0)

    if final:
        tile_of = lambda i: (i // lat) * per + ctx_tiles + i % lat
        steps = (r // tt // per) * lat
        mod_map = lambda i, ts: (i // lat, 0, 0)
        out_specs = pl.BlockSpec((tt, d), lambda i, ts: (i, 0))
        out_shape = jax.ShapeDtypeStruct((steps * tt, d), F32)
    else:
        tile_of = lambda i: i
        steps = r // tt
        mod_map = lambda i, ts: (jnp.where(i % per < ctx_tiles, 2, i // per), 0, 0)
        out_specs = [pl.BlockSpec((tt, d), lambda i, ts: (i, 0))] * 2
        out_shape = [jax.ShapeDtypeStruct((r, d), F32), jax.ShapeDtypeStruct((r, d), BF16)]
    row = lambda i, ts: (tile_of(i), 0)
    grid_spec = pltpu.PrefetchScalarGridSpec(
        num_scalar_prefetch=1,
        grid=(steps,),
        in_specs=[pl.BlockSpec((tt, d), row), pl.BlockSpec(memory_space=pl.ANY),
                  pl.BlockSpec((tt, 1), row), pl.BlockSpec((tt, 1), row),
                  pl.BlockSpec((1, 6, d), mod_map), pl.BlockSpec((1, 6, d), mod_map),
                  pl.BlockSpec((1, d), const)],
        out_specs=out_specs,
        scratch_shapes=[pltpu.VMEM((2, smax + COMBINE_CHUNK, d), F32), pltpu.VMEM((tt, d), F32),
                        pltpu.SemaphoreType.DMA((2,))],
    )
    return pl.pallas_call(
        functools.partial(_combine_kernel, final=final, smax=smax, tile_of=tile_of),
        grid_spec=grid_spec,
        out_shape=out_shape,
        compiler_params=_cparams(("arbitrary",)),
        name="moe_combine",
    )(tile_start, h, y, start, count, mods, mods_next, g.reshape(1, d))


def _route(aff, b, l, tc, with_ctx):
    aff3 = aff.reshape(b, l, N_EXPERTS)
    ne = N_EXPERTS
    ids, gates = [], []
    sel = jnp.zeros((b, l, ne), jnp.bool_)
    segs = [(tc, l - tc)] + ([(0, tc)] if with_ctx else [])
    for start, n in segs:
        cap = EC_CAPACITY * n // ne
        seg_aff = jnp.swapaxes(aff3[:, start:start + n], 1, 2)
        _, idx = lax.top_k(seg_aff, cap)
        idx = jnp.sort(idx, axis=-1)
        g = jnp.take_along_axis(seg_aff, idx, axis=-1)
        hit = jnp.any(idx[..., None] == jnp.arange(n, dtype=idx.dtype), axis=2)
        sel = sel.at[:, start:start + n].set(jnp.swapaxes(hit, 1, 2))
        rows = idx + start + (jnp.arange(b, dtype=jnp.int32) * l)[:, None, None]
        ids.append(jnp.swapaxes(rows, 0, 1).reshape(ne, b * cap))
        gates.append(jnp.swapaxes(g, 0, 1).reshape(ne, b * cap))
    idx_all = jnp.concatenate(ids, 1).astype(jnp.int32)
    gate_all = jnp.concatenate(gates, 1)[..., None]
    sel_i = sel.reshape(b * l, ne).astype(jnp.int32)
    count = jnp.sum(sel_i, axis=1)
    start_row = jnp.cumsum(count) - count
    slot = start_row[:, None] + jnp.cumsum(sel_i, axis=1) - sel_i
    dst_all = jnp.take_along_axis(slot.T, idx_all, axis=1).astype(jnp.int32)
    tile_n = jnp.sum(count.reshape(-1, COMBINE_TILE), axis=1)
    tile_info = jnp.stack([start_row[::COMBINE_TILE], tile_n]).astype(jnp.int32)
    return idx_all, gate_all, dst_all, tile_info, start_row[:, None].astype(jnp.int32), \
        count[:, None].astype(jnp.int32)


def kernel(x, c, ctx, c_ctx, w_mod, b_mod, norm1_g, norm2_g, w_in, gdn_conv_w, gdn_a_log, gdn_dt_bias,
           gdn_norm_g, sc_conv_w, diff_lambda, diff_norm_g, w_out, w_router, w_e_gate, w_e_up, w_e_down,
           final_norm_g):
    b, t, d = x.shape
    tc = ctx.shape[1]
    l = tc + t
    r = b * l
    depth = w_mod.shape[0]
    tiles = l // ROW_TILE
    nh = GDN_HEADS

    cond = jnp.concatenate([c, c_ctx[None, :], jnp.zeros((8 - b - 1, d), F32)], 0)
    mods_all = _modulation(cond, w_mod, b_mod)[:, :3].reshape(depth, 3, 6, d)
    cos_t, sin_t = _rope_tables(t, tc)
    a_in = 4 * GDN_W
    w_main = jnp.concatenate([w_in[:, :, :a_in], w_in[:, :, a_in + 4 * nh:]], 2).astype(BF16)
    w_ba = jnp.concatenate([w_in[:, :, a_in:a_in + 4 * nh], jnp.zeros((depth, d, LANE - 4 * nh), F32)],
                           2).astype(BF16)
    w_out_b = w_out.astype(BF16)

    h = jnp.concatenate([ctx, x], 1).reshape(r, d)
    a = _norm_mod(h, norm1_g[0], mods_all[0], tiles)
    out = None
    for li in range(depth):
        with_ctx = li < depth - 1
        lam_init = 0.8 - 0.6 * math.exp(-0.3 * li)
        u = _matmul(a, w_main, li, BF16, r // 8, GDN_W)
        ba = _matmul(a, w_ba, li, F32, r // 8, LANE)
        u3 = u.reshape(b, l, N_MAIN)

        qkv = _gdn_prep(u3, gdn_conv_w[li], tc)
        prm = jnp.zeros((8, LANE), F32)
        prm = prm.at[0, 2 * nh:4 * nh].set(-jnp.exp(gdn_a_log[li].reshape(-1)))
        prm = prm.at[1, 2 * nh:4 * nh].set(gdn_dt_bias[li].reshape(-1))
        o_f, o_b = _gdn(qkv, ba.reshape(b, l, LANE), prm, tc, GDN_CHUNK)
        y_b = _short_conv(u3, sc_conv_w[li], tc)
        q_t, k_r, v_t = _attn_prep(u3, cos_t, sin_t)
        y_c = _diff_attn(q_t, k_r, v_t, diff_lambda[li], diff_norm_g[li], lam_init, tc)

        h, a2, aff = _outproj(o_f.reshape(r, GDN_W), o_b.reshape(r, GDN_W), u, y_b.reshape(r, SC_W),
                              y_c.reshape(r, DIFF_W), w_out_b, li, h, mods_all[li],
                              gdn_norm_g[li], norm2_g[li], w_router[li], tiles)
        idx, gate, dst, tile_start, start_row, count = _route(aff, b, l, tc, with_ctx)
        y = _moe(idx, dst, a2, gate, w_e_gate, w_e_up, w_e_down, li)
        if with_ctx:
            h, a = _combine(h, y, tile_start, start_row, count, mods_all[li], mods_all[li + 1],
                            norm1_g[li + 1], tiles, tc, final=False)
        else:
            out = _combine(h, y, tile_start, start_row, count, mods_all[li], mods_all[li],
                           final_norm_g, tiles, tc, final=True)
    return out.reshape(b, t, d)
```

```python
import functools
import math

import numpy as np
import jax
import jax.numpy as jnp
from jax import lax
from jax.experimental import pallas as pl
from jax.experimental.pallas import tpu as pltpu

F32 = jnp.float32
BF16 = jnp.bfloat16

EPS = 1e-6
GRID_W = 64
GDN_HEADS = 6
GDN_D = 128
GDN_W = GDN_HEADS * GDN_D
SC_W = 512
DIFF_HEADS = 6
DIFF_DH = 64
DIFF_DV = 128
DIFF_W = DIFF_HEADS * DIFF_DV
ROPE_BASE = 10000.0
ROPE_PAIRS = DIFF_DH // 4
N_EXPERTS = 16
EC_CAPACITY = 2

COL_Z = 3 * GDN_W
COL_SC = COL_Z + GDN_W
COL_ATT = COL_SC + 3 * SC_W
N_MAIN = COL_ATT + 3 * DIFF_W
LANE = 128
ROW_TILE = 256
GDN_CHUNK = 128
VT_ROWS = DIFF_DV + 16
VMEM_LIMIT = 56 * 1024 * 1024


def _cparams(sem, vmem=VMEM_LIMIT):
    return pltpu.CompilerParams(dimension_semantics=sem, vmem_limit_bytes=vmem)


def _sigmoid(x):
    return 1.0 / (1.0 + jnp.exp(-x))


def _silu(x):
    return x * _sigmoid(x)


def _softplus(x):
    return jnp.maximum(x, 0.0) + jnp.log(1.0 + jnp.exp(-jnp.abs(x)))


def _rms(x, g):
    return x * lax.rsqrt(jnp.mean(x * x, axis=-1, keepdims=True) + EPS) * g


def _dot(a, b, **kw):
    return jnp.dot(a, b, preferred_element_type=F32, **kw)


def _dot_nt(a, b):
    return lax.dot_general(a, b, (((1,), (1,)), ((), ())), preferred_element_type=F32)


def _mod_kernel(s_ref, w_ref, b_ref, o_ref):
    k = pl.program_id(1)
    tk = w_ref.shape[1]

    @pl.when(k == 0)
    def _():
        o_ref[0] = jnp.broadcast_to(b_ref[0], o_ref.shape[1:])

    s = _silu(s_ref[:, pl.ds(pl.multiple_of(k * tk, tk), tk)]).astype(BF16)
    o_ref[0] += _dot(s, w_ref[0].astype(BF16))


def _modulation(cond, w_mod, b_mod):
    depth, d, n = w_mod.shape
    tk = 256
    return pl.pallas_call(
        _mod_kernel,
        grid=(depth, d // tk),
        in_specs=[pl.BlockSpec((8, d), lambda l, k: (0, 0)),
                  pl.BlockSpec((1, tk, n), lambda l, k: (l, k, 0)),
                  pl.BlockSpec((1, 1, n), lambda l, k: (l, 0, 0))],
        out_specs=pl.BlockSpec((1, 8, n), lambda l, k: (l, 0, 0)),
        out_shape=jax.ShapeDtypeStruct((depth, 8, n), F32),
        compiler_params=_cparams(("arbitrary", "arbitrary")),
        name="modulation",
    )(cond, w_mod, b_mod.reshape(depth, 1, n))


def _mod_row_map(tiles_per_sample):
    def index_map(i):
        return (jnp.where(i % tiles_per_sample == 0, 2, i // tiles_per_sample), 0, 0)
    return index_map


def _norm_mod_kernel(h_ref, g_ref, m_ref, a_ref):
    m = m_ref[0]
    a = _rms(h_ref[...], g_ref[...]) * (1.0 + m[1:2]) + m[0:1]
    a_ref[...] = a.astype(a_ref.dtype)


def _norm_mod(h, g, mods, tiles_per_sample):
    r, d = h.shape
    return pl.pallas_call(
        _norm_mod_kernel,
        grid=(r // ROW_TILE,),
        in_specs=[pl.BlockSpec((ROW_TILE, d), lambda i: (i, 0)),
                  pl.BlockSpec((1, d), lambda i: (0, 0)),
                  pl.BlockSpec((1, 6, d), _mod_row_map(tiles_per_sample))],
        out_specs=pl.BlockSpec((ROW_TILE, d), lambda i: (i, 0)),
        out_shape=jax.ShapeDtypeStruct((r, d), BF16),
        compiler_params=_cparams(("arbitrary",)),
        name="norm_mod",
    )(h, g.reshape(1, d), mods)


def _mm_kernel(a_ref, w_ref, o_ref):
    o_ref[...] = _dot(a_ref[...], w_ref[0]).astype(o_ref.dtype)


def _matmul(a, w, li, out_dtype, tm, tn):
    r, k = a.shape
    n = w.shape[2]
    return pl.pallas_call(
        _mm_kernel,
        grid=(n // tn, r // tm),
        in_specs=[pl.BlockSpec((tm, k), lambda j, i: (i, 0)),
                  pl.BlockSpec((1, k, tn), lambda j, i: (li, 0, j))],
        out_specs=pl.BlockSpec((tm, tn), lambda j, i: (i, j)),
        out_shape=jax.ShapeDtypeStruct((r, n), out_dtype),
        compiler_params=_cparams(("arbitrary", "arbitrary")),
        name="in_proj",
    )(a, w)


def _conv3(x, w, tc):
    l = x.shape[0]
    row = lax.broadcasted_iota(jnp.int32, x.shape, 0)
    prev = jnp.where((row == 0) | (row == tc), 0.0, pltpu.roll(x, 1, 0))
    nxt = jnp.where((row == tc - 1) | (row == l - 1), 0.0, pltpu.roll(x, l - 1, 0))
    return prev * w[0:1] + x * w[1:2] + nxt * w[2:3]


def _gdn_prep_kernel(u_ref, w_ref, o_ref, *, tc):
    j = pl.program_id(1)
    y = _silu(_conv3(u_ref[0].astype(F32), w_ref[...], tc))
    inv = lax.rsqrt(jnp.sum(y * y, axis=-1, keepdims=True) + EPS)
    nh = GDN_HEADS
    fac = jnp.where(j < nh, inv * GDN_D ** -0.5, jnp.where(j < 2 * nh, inv, 1.0))
    o_ref[0] = (y * fac).astype(o_ref.dtype)


def _gdn_prep(u3, conv_w, tc):
    b, l, _ = u3.shape
    nblk = 3 * GDN_W // LANE
    return pl.pallas_call(
        functools.partial(_gdn_prep_kernel, tc=tc),
        grid=(b, nblk),
        in_specs=[pl.BlockSpec((1, l, LANE), lambda bi, j: (bi, 0, j)),
                  pl.BlockSpec((3, LANE), lambda bi, j: (0, j))],
        out_specs=pl.BlockSpec((1, l, LANE), lambda bi, j: (bi, 0, j)),
        out_shape=jax.ShapeDtypeStruct((b, l, 3 * GDN_W), BF16),
        compiler_params=_cparams(("arbitrary", "arbitrary")),
        name="gdn_prep",
    )(u3, conv_w)


def _sconv_kernel(b_ref, c_ref, x_ref, w_ref, o_ref, *, tc):
    inner = c_ref[0].astype(F32) * x_ref[0].astype(F32)
    o_ref[0] = (b_ref[0].astype(F32) * _conv3(inner, w_ref[...], tc)).astype(o_ref.dtype)


def _short_conv(u3, conv_w, tc):
    b, l, _ = u3.shape
    nblk = SC_W // LANE
    base = COL_SC // LANE

    def spec(k):
        return pl.BlockSpec((1, l, LANE), lambda bi, j: (bi, 0, base + k * nblk + j))

    return pl.pallas_call(
        functools.partial(_sconv_kernel, tc=tc),
        grid=(b, nblk),
        in_specs=[spec(0), spec(1), spec(2), pl.BlockSpec((3, LANE), lambda bi, j: (0, j))],
        out_specs=pl.BlockSpec((1, l, LANE), lambda bi, j: (bi, 0, j)),
        out_shape=jax.ShapeDtypeStruct((b, l, SC_W), BF16),
        compiler_params=_cparams(("arbitrary", "arbitrary")),
        name="short_conv",
    )(u3, u3, u3, conv_w)


def _rope(x, cos, sin):
    lane = lax.broadcasted_iota(jnp.int32, x.shape, 1)
    swapped = jnp.where(lane % 32 < 16, pltpu.roll(x, LANE - 16, 1), pltpu.roll(x, 16, 1))
    return x * cos + swapped * sin


def _attn_prep_kernel(q_ref, k_ref, v_ref, cos_ref, sin_ref, qt_ref, ko_ref, vt_ref):
    cos = cos_ref[...]
    sin = sin_ref[...]
    q = _rope(q_ref[0].astype(F32), cos, sin) * DIFF_DH ** -0.5
    qt_ref[0, 0] = q.T.astype(qt_ref.dtype)
    ko_ref[0] = _rope(k_ref[0].astype(F32), cos, sin).astype(ko_ref.dtype)
    vt_ref[0, 0, 0:DIFF_DV, :] = v_ref[0].astype(F32).T.astype(vt_ref.dtype)
    vt_ref[0, 0, DIFF_DV:, :] = jnp.ones((VT_ROWS - DIFF_DV, v_ref.shape[1]), vt_ref.dtype)


def _attn_prep(u3, cos_t, sin_t):
    b, l, _ = u3.shape
    nh = DIFF_HEADS
    base = COL_ATT // LANE
    tbl = pl.BlockSpec((l, LANE), lambda bi, j: (0, 0))
    tspec = pl.BlockSpec((1, 1, LANE, l), lambda bi, j: (bi, j, 0, 0))
    tshape = jax.ShapeDtypeStruct((b, nh, LANE, l), BF16)
    vspec = pl.BlockSpec((1, 1, VT_ROWS, l), lambda bi, j: (bi, j, 0, 0))
    vshape = jax.ShapeDtypeStruct((b, nh, VT_ROWS, l), BF16)
    return pl.pallas_call(
        _attn_prep_kernel,
        grid=(b, nh),
        in_specs=[pl.BlockSpec((1, l, LANE), lambda bi, j: (bi, 0, base + j)),
                  pl.BlockSpec((1, l, LANE), lambda bi, j: (bi, 0, base + nh + j)),
                  pl.BlockSpec((1, l, LANE), lambda bi, j: (bi, 0, base + 2 * nh + j)),
                  tbl, tbl],
        out_specs=[tspec, pl.BlockSpec((1, l, LANE), lambda bi, j: (bi, 0, j)), vspec],
        out_shape=[tshape, jax.ShapeDtypeStruct((b, l, DIFF_W), BF16), vshape],
        compiler_params=_cparams(("arbitrary", "arbitrary")),
        name="attn_prep",
    )(u3, u3, u3, cos_t, sin_t)


def _rope_tables(t, tc):
    rows = t // GRID_W
    row = np.repeat(np.arange(rows, dtype=np.float32), GRID_W)
    col = np.tile(np.arange(GRID_W, dtype=np.float32), rows)
    inv = (np.float32(ROPE_BASE) ** (-np.arange(ROPE_PAIRS, dtype=np.float32) / np.float32(ROPE_PAIRS))).astype(np.float32)
    lane = np.arange(LANE)
    axis = (lane // 32) % 2
    pos = np.where(axis[None, :] == 0, row[:, None], col[:, None]).astype(np.float32)
    ang = (pos * inv[lane % 16][None, :]).astype(np.float32)
    sign = np.where(lane % 32 < 16, -1.0, 1.0)[None, :].astype(np.float32)
    cos_t = np.concatenate([np.ones((tc, LANE), np.float32), np.cos(ang).astype(np.float32)], 0)
    sin_t = np.concatenate([np.zeros((tc, LANE), np.float32), (np.sin(ang) * sign).astype(np.float32)], 0)
    return jnp.asarray(cos_t), jnp.asarray(sin_t)


def _gdn_kernel(qf_ref, kf_ref, vf_ref, gf_ref, qb_ref, kb_ref, vb_ref, gb_ref, prm_ref,
                of_ref, ob_ref, s_ref, *, chunk):
    c = chunk
    nh = GDN_HEADS

    @pl.when(pl.program_id(1) == 0)
    def _():
        s_ref[...] = jnp.zeros_like(s_ref)

    ii = lax.broadcasted_iota(jnp.int32, (c, c), 0)
    jj = lax.broadcasted_iota(jnp.int32, (c, c), 1)
    neg_a = prm_ref[0:1, :]
    dt_b = prm_ref[1:2, :]
    blk = [(ii >> sh) == (jj >> sh) for sh in range(3, int(math.log2(c)) + 1)]
    dirs = ((qf_ref, kf_ref, vf_ref, gf_ref, of_ref), (qb_ref, kb_ref, vb_ref, gb_ref, ob_ref))
    offm = [blk[lvl] & ~blk[lvl - 1] for lvl in range(1, len(blk))]

    units = []
    for d, (q_ref, k_ref, v_ref, g_ref, o_ref) in enumerate(dirs):
        incl = (ii >= jj) if d == 0 else (ii <= jj)
        strict = (ii > jj) if d == 0 else (ii < jj)
        raw = g_ref[0]
        g_all = neg_a * _softplus(raw + dt_b)
        beta_all = _sigmoid(raw)
        tri = incl.astype(F32)
        cum = _dot(tri, g_all, precision=lax.Precision.HIGHEST)
        cum_t = _dot(g_all.T, tri.T, precision=lax.Precision.HIGHEST)
        last = c - 1 if d == 0 else 0
        for h in range(nh):
            cb = d * nh + h
            cg = 2 * nh + cb
            sl = slice(h * GDN_D, (h + 1) * GDN_D)
            cc = cum[:, cg:cg + 1]
            units.append(dict(
                cb=cb, sl=sl, o_ref=o_ref, incl=incl, strict=strict, cc=cc,
                beta=beta_all[:, cb:cb + 1], cr=cum_t[cg:cg + 1, :], tot=cum[last:last + 1, cg:cg + 1],
                q=q_ref[0, :, sl], k=k_ref[0, :, sl], v=v_ref[0, :, sl]))

    for un in units:
        un["kbeta"] = un["k"].astype(F32) * un["beta"]
        un["gram"] = _dot_nt(jnp.concatenate([un["kbeta"].astype(BF16), un["q"]], axis=0), un["k"])
    for un in units:
        decay = jnp.exp(jnp.where(un["incl"], un["cc"] - un["cr"], -1e30))
        un["low"] = jnp.where(un["strict"], un["gram"][:c] * decay, 0.0)
        un["attn"] = (un["gram"][c:] * decay).astype(BF16)
        un["p"] = jnp.where(blk[0], -un["low"], 0.0)
        un["n"] = un["p"]
    for _ in range(2):
        for un in units:
            pb = un["p"].astype(BF16)
            un["p"] = _dot(pb, pb)
        for un in units:
            un["n"] = un["n"] + un["p"] + _dot(un["n"].astype(BF16), un["p"].astype(BF16))
    for om in offm:
        for un in units:
            off = jnp.where(om, un["low"], 0.0).astype(BF16)
            un["m1"] = off + _dot(off, un["n"].astype(BF16))
        for un in units:
            un["n"] = un["n"] - un["m1"] - _dot(un["n"].astype(BF16), un["m1"].astype(BF16))
    for un in units:
        eg = jnp.exp(un["cc"])
        rhs = jnp.concatenate([un["v"].astype(F32) * un["beta"], un["kbeta"] * eg], axis=1)
        sol = rhs + _dot(un["n"].astype(BF16), rhs.astype(BF16))
        un["u"] = sol[:, :GDN_D]
        qd = un["q"].astype(F32) * eg
        un["wq"] = jnp.concatenate([sol[:, GDN_D:].astype(BF16), qd.astype(BF16)], axis=0)
        un["kdt"] = (un["k"].astype(F32) * jnp.exp(un["tot"] - un["cc"])).T.astype(BF16)
    for un in units:
        un["s"] = s_ref[un["cb"]]
        un["ws"] = _dot(un["wq"], un["s"].astype(BF16))
    for un in units:
        un["v_new"] = (un["u"] - un["ws"][:c]).astype(BF16)
        un["o_ref"][0, :, un["sl"]] = un["ws"][c:] + _dot(un["attn"], un["v_new"])
    for un in units:
        s_ref[un["cb"]] = un["s"] * jnp.exp(un["tot"]) + _dot(un["kdt"], un["v_new"])


def _gdn(qkv, ba, prm, tc, chunk):
    b, l, _ = qkv.shape
    nc = l // chunk
    ncc = tc // chunk

    def fwd(col):
        return lambda bi, n: (bi, n, col)

    def bwd(col):
        return lambda bi, n: (bi, jnp.where(n < ncc, ncc - 1 - n, nc - 1 + ncc - n), col)

    blk = (1, chunk, GDN_W)
    gblk = (1, chunk, LANE)
    out = jax.ShapeDtypeStruct((b, l, GDN_W), F32)
    return pl.pallas_call(
        functools.partial(_gdn_kernel, chunk=chunk),
        grid=(b, nc),
        in_specs=[pl.BlockSpec(blk, fwd(0)), pl.BlockSpec(blk, fwd(1)), pl.BlockSpec(blk, fwd(2)),
                  pl.BlockSpec(gblk, fwd(0)),
                  pl.BlockSpec(blk, bwd(0)), pl.BlockSpec(blk, bwd(1)), pl.BlockSpec(blk, bwd(2)),
                  pl.BlockSpec(gblk, bwd(0)),
                  pl.BlockSpec((8, LANE), lambda bi, n: (0, 0))],
        out_specs=[pl.BlockSpec(blk, fwd(0)), pl.BlockSpec(blk, bwd(0))],
        out_shape=[out, out],
        scratch_shapes=[pltpu.VMEM((2 * GDN_HEADS, GDN_D, GDN_D), F32)],
        compiler_params=_cparams(("arbitrary", "arbitrary")),
        name="gdn",
    )(qkv, qkv, qkv, ba, qkv, qkv, qkv, ba, prm)


def _attn_kernel(qt_ref, k_ref, vt_ref, lp_ref, g_ref, o_ref, s_sc, acc_sc, *, tq, tk, nk_ctx, nk_all, lam_init):
    qt = qt_ref[0, 0]
    sub = lax.broadcasted_iota(jnp.int32, qt.shape, 0)
    zero = jnp.zeros_like(qt)
    qq = jnp.concatenate([jnp.where(sub < DIFF_DH, qt, zero), jnp.where(sub >= DIFF_DH, qt, zero)], axis=1)
    lp = lp_ref[...]
    lam = (jnp.exp(jnp.sum(lp[0:1] * lp[1:2], axis=1, keepdims=True))
           - jnp.exp(jnp.sum(lp[2:3] * lp[3:4], axis=1, keepdims=True)) + lam_init)

    def run(nkv):
        n = nkv * tk
        s_sc[0:n, :] = _dot(k_ref[0, 0:n, :], qq)
        m = jnp.full((1, 2 * tq), -jnp.inf, F32)
        for c in range(nkv):
            s = s_sc[c * tk:(c + 1) * tk, :]
            m_new = jnp.maximum(m, jnp.max(s, axis=0, keepdims=True))
            p = jnp.exp(s - m_new).astype(BF16)
            pv = _dot(vt_ref[0, 0, :, c * tk:(c + 1) * tk], p)
            acc_sc[...] = pv if c == 0 else jnp.exp(m - m_new) * acc_sc[...] + pv
            m = m_new
        on = acc_sc[0:DIFF_DV, :] / acc_sc[DIFF_DV:DIFF_DV + 1, :]
        o = on[:, :tq] - lam * on[:, tq:]
        y = o * lax.rsqrt(jnp.mean(o * o, axis=0, keepdims=True) + EPS) * g_ref[...] * (1.0 - lam_init)
        o_ref[0] = y.T.astype(o_ref.dtype)

    is_ctx = pl.program_id(2) == 0

    @pl.when(is_ctx)
    def _():
        run(nk_ctx)

    @pl.when(jnp.logical_not(is_ctx))
    def _():
        run(nk_all)


def _diff_attn(qt, k, vt, lam_params, norm_g, lam_init, tc):
    b, l, _ = k.shape
    tq = ROW_TILE
    tk = 256
    kern = functools.partial(_attn_kernel, tq=tq, tk=tk, nk_ctx=tc // tk, nk_all=l // tk, lam_init=lam_init)
    return pl.pallas_call(
        kern,
        grid=(b, DIFF_HEADS, l // tq),
        in_specs=[pl.BlockSpec((1, 1, LANE, tq), lambda bi, h, i: (bi, h, 0, i)),
                  pl.BlockSpec((1, l, LANE), lambda bi, h, i: (bi, 0, h)),
                  pl.BlockSpec((1, 1, VT_ROWS, l), lambda bi, h, i: (bi, h, 0, 0)),
                  pl.BlockSpec((4, DIFF_DH), lambda bi, h, i: (0, 0)),
                  pl.BlockSpec((DIFF_DV, 1), lambda bi, h, i: (0, 0))],
        out_specs=pl.BlockSpec((1, tq, LANE), lambda bi, h, i: (bi, i, h)),
        out_shape=jax.ShapeDtypeStruct((b, l, DIFF_W), BF16),
        scratch_shapes=[pltpu.VMEM((l, 2 * tq), F32), pltpu.VMEM((VT_ROWS, 2 * tq), F32)],
        compiler_params=_cparams(("arbitrary", "arbitrary", "arbitrary")),
        name="diff_attn",
    )(qt, k, vt, lam_params, norm_g.reshape(DIFF_DV, 1))


def _outproj_kernel(of_ref, ob_ref, z_ref, yb_ref, yc_ref, w_ref, h_ref, m_ref, gn_ref, g2_ref, wr_ref,
                    h_out, a_out, aff_out):
    o = of_ref[...] + ob_ref[...]
    gn = gn_ref[...]
    parts = []
    for hd in range(GDN_HEADS):
        sl = slice(hd * GDN_D, (hd + 1) * GDN_D)
        parts.append((_rms(o[:, sl], gn) * _silu(z_ref[:, sl].astype(F32))).astype(BF16))
    ya = jnp.concatenate(parts, axis=1)
    acc = _dot(ya, w_ref[0, 0:GDN_W, :])
    acc += _dot(yb_ref[...], w_ref[0, GDN_W:GDN_W + SC_W, :])
    acc += _dot(yc_ref[...], w_ref[0, GDN_W + SC_W:, :])
    m = m_ref[0]
    hn = h_ref[...] + m[2:3] * acc
    h_out[...] = hn
    a = _rms(hn, g2_ref[...]) * (1.0 + m[4:5]) + m[3:4]
    a_out[...] = a
    a_hi = a.astype(BF16)
    a_lo = (a - a_hi.astype(F32)).astype(BF16)
    wr = wr_ref[...]
    w_hi = wr.astype(BF16)
    w_lo = (wr - w_hi.astype(F32)).astype(BF16)
    prod = _dot(jnp.concatenate([a_hi, a_lo], axis=0), jnp.concatenate([w_hi, w_lo], axis=1))
    tm, ne = a.shape[0], wr.shape[1]
    logits = (prod[:tm, :ne] + prod[:tm, ne:]) + (prod[tm:, :ne] + prod[tm:, ne:])
    e = jnp.exp(logits - jnp.max(logits, axis=-1, keepdims=True))
    aff_out[...] = e / jnp.sum(e, axis=-1, keepdims=True)


def _outproj(o_f, o_b, u, y_b, y_c, w_out, li, h, mods, gdn_g, g2, w_router, tiles_per_sample):
    r, d = h.shape
    tm = ROW_TILE
    zb = COL_Z // GDN_W
    row = lambda i: (i, 0)
    const = lambda i: (0, 0)
    return pl.pallas_call(
        _outproj_kernel,
        grid=(r // tm,),
        in_specs=[pl.BlockSpec((tm, GDN_W), row), pl.BlockSpec((tm, GDN_W), row),
                  pl.BlockSpec((tm, GDN_W), lambda i: (i, zb)),
                  pl.BlockSpec((tm, SC_W), row), pl.BlockSpec((tm, DIFF_W), row),
                  pl.BlockSpec((1, d, d), lambda i: (li, 0, 0)), pl.BlockSpec((tm, d), row),
                  pl.BlockSpec((1, 6, d), _mod_row_map(tiles_per_sample)),
                  pl.BlockSpec((1, GDN_D), const), pl.BlockSpec((1, d), const),
                  pl.BlockSpec((d, N_EXPERTS), const)],
        out_specs=[pl.BlockSpec((tm, d), row), pl.BlockSpec((tm, d), row),
                   pl.BlockSpec((tm, N_EXPERTS), row)],
        out_shape=[jax.ShapeDtypeStruct((r, d), F32), jax.ShapeDtypeStruct((r, d), F32),
                   jax.ShapeDtypeStruct((r, N_EXPERTS), F32)],
        compiler_params=_cparams(("arbitrary",)),
        name="out_proj",
    )(o_f, o_b, u, y_b, y_c, w_out, h, mods, gdn_g.reshape(1, GDN_D), g2.reshape(1, d), w_router)


def _moe_kernel(idx_ref, dst_ref, a_hbm, gate_ref, wg_ref, wu_ref, wd_ref, y_hbm, x32, xb, acc, gsem, ssem,
                *, nrows, nf):
    e = pl.program_id(0)
    f = pl.program_id(1)
    ne = pl.num_programs(0)
    slot = e % 2

    def gather(expert):
        def issue(g, carry):
            r0 = pl.multiple_of(g * 8, 8)
            for k in range(8):
                pltpu.make_async_copy(a_hbm.at[pl.ds(idx_ref[expert * nrows + r0 + k], 1), :],
                                      x32.at[pl.ds(r0 + k, 1), :], gsem.at[0]).start(priority=k % 2)
            return carry
        lax.fori_loop(0, nrows // 8, issue, 0)

    def scatter_wait(s):
        pltpu.make_async_copy(acc.at[s], y_hbm.at[pl.ds(0, nrows), :], ssem.at[s]).wait()

    @pl.when((e == 0) & (f == 0))
    def _():
        gather(0)

    @pl.when(f == 0)
    def _():
        pltpu.make_async_copy(a_hbm.at[pl.ds(0, nrows), :], x32, gsem.at[0]).wait()
        xb[...] = x32[...].astype(BF16)

    @pl.when((f == 0) & (e + 1 < ne))
    def _():
        gather(e + 1)

    @pl.when((f == 0) & (e >= 2))
    def _():
        scatter_wait(slot)

    x = xb[...]
    hid = _silu(_dot(x, wg_ref[0, 0].astype(BF16))) * _dot(x, wu_ref[0, 0].astype(BF16))
    part = _dot(hid.astype(BF16), wd_ref[0, 0].astype(BF16))

    @pl.when(f == 0)
    def _():
        acc[slot] = part

    @pl.when(f > 0)
    def _():
        acc[slot] += part

    @pl.when(f == nf - 1)
    def _():
        acc[slot] = acc[slot] * gate_ref[0]

        def issue(g, carry):
            r0 = pl.multiple_of(g * 8, 8)
            for k in range(8):
                pltpu.make_async_copy(acc.at[slot, pl.ds(r0 + k, 1), :],
                                      y_hbm.at[pl.ds(dst_ref[e * nrows + r0 + k], 1), :],
                                      ssem.at[slot]).start(priority=k % 2)
            return carry
        lax.fori_loop(0, nrows // 8, issue, 0)

    @pl.when((f == nf - 1) & (e == ne - 1))
    def _():
        if ne >= 2:
            scatter_wait(1 - slot)
        scatter_wait(slot)


def _moe(idx, dst, a, gate, w_gate, w_up, w_down, li):
    ne, nrows = idx.shape
    d = a.shape[1]
    ff = w_gate.shape[3]
    tf = 256
    nf = ff // tf
    grid_spec = pltpu.PrefetchScalarGridSpec(
        num_scalar_prefetch=2,
        grid=(ne, nf),
        in_specs=[pl.BlockSpec(memory_space=pl.ANY),
                  pl.BlockSpec((1, nrows, 1), lambda e, f, i_ref, d_ref: (e, 0, 0)),
                  pl.BlockSpec((1, 1, d, tf), lambda e, f, i_ref, d_ref: (li, e, 0, f)),
                  pl.BlockSpec((1, 1, d, tf), lambda e, f, i_ref, d_ref: (li, e, 0, f)),
                  pl.BlockSpec((1, 1, tf, d), lambda e, f, i_ref, d_ref: (li, e, f, 0))],
        out_specs=pl.BlockSpec(memory_space=pl.ANY),
        scratch_shapes=[pltpu.VMEM((nrows, d), F32), pltpu.VMEM((nrows, d), BF16),
                        pltpu.VMEM((2, nrows, d), F32),
                        pltpu.SemaphoreType.DMA((1,)), pltpu.SemaphoreType.DMA((2,))],
    )
    return pl.pallas_call(
        functools.partial(_moe_kernel, nrows=nrows, nf=nf),
        grid_spec=grid_spec,
        out_shape=jax.ShapeDtypeStruct((ne * nrows, d), F32),
        compiler_params=_cparams(("arbitrary", "arbitrary")),
        name="moe_experts",
    )(idx.reshape(-1), dst.reshape(-1), a, gate, w_gate, w_up, w_down)


COMBINE_TILE = 128
COMBINE_CHUNK = 256


def _combine_kernel(ts_ref, h_ref, y_hbm, st_ref, cn_ref, m_ref, mn_ref, g_ref, *rest, final, smax, tile_of):
    if final:
        o_ref, stage, acc, sem = rest
    else:
        h_out, a_out, stage, acc, sem = rest
    step = pl.program_id(0)
    nt = pl.num_programs(0)
    i = tile_of(step)
    tt = h_ref.shape[0]
    buf = step % 2

    def tile_range(t):
        return pl.multiple_of(ts_ref[0, t] & ~7, 8), ts_ref[1, t] + (ts_ref[0, t] & 7)

    def transfer(t, b, wait):
        s, n = tile_range(t)
        pieces = [(1 << k, (n & (1 << k)) != 0, pl.multiple_of(n & ~((2 << k) - 1), 8))
                  for k in range(int(math.log2(smax)), 2, -1)]
        pieces += [(1, j < (n & 7), (n & ~7) + j) for j in range(7)]
        for p, cond, off in pieces:
            @pl.when(cond)
            def _(p=p, off=off):
                src = s + off if p == 1 else pl.multiple_of(s + off, 8)
                cp = pltpu.make_async_copy(y_hbm.at[pl.ds(src, p), :], stage.at[b, pl.ds(off, p), :], sem.at[b])
                if wait:
                    cp.wait()
                else:
                    cp.start()

    @pl.when(step == 0)
    def _():
        stage[...] = jnp.zeros_like(stage)
        transfer(i, 0, wait=False)

    @pl.when(step + 1 < nt)
    def _():
        transfer(tile_of(step + 1), 1 - buf, wait=False)

    transfer(i, buf, wait=True)
    s, n = tile_range(i)
    acc[...] = jnp.zeros_like(acc)
    lo = st_ref[...] - s
    hi = lo + cn_ref[...]

    def body(c, carry):
        base = pl.multiple_of(c * COMBINE_CHUNK, COMBINE_CHUNK)
        jj = base + lax.broadcasted_iota(jnp.int32, (tt, COMBINE_CHUNK), 1)
        seg = jnp.where((jj >= lo) & (jj < hi), 1.0, 0.0).astype(BF16)
        acc[...] += _dot(seg, stage[buf, pl.ds(base, COMBINE_CHUNK), :].astype(BF16))
        return carry

    lax.fori_loop(0, (n + COMBINE_CHUNK - 1) // COMBINE_CHUNK, body, 0)
    hn = h_ref[...] + m_ref[0][5:6] * acc[...]
    if final:
        o_ref[...] = _rms(hn, g_ref[...])
    else:
        h_out[...] = hn
        mn = mn_ref[0]
        a_out[...] = (_rms(hn, g_ref[...]) * (1.0 + mn[1:2]) + mn[0:1]).astype(a_out.dtype)


def _combine(h, y, tile_start, start, count, mods, mods_next, g, tiles_per_sample, ctx_rows, final):
    r, d = h.shape
    tt = COMBINE_TILE
    smax = N_EXPERTS * tt
    per = tiles_per_sample * (ROW_TILE // tt)
    ctx_tiles = ctx_rows // tt
    lat = per - ctx_tiles
    const = lambda i, ts: (0, 0)

    if final:
        tile_of = lambda i: (i // lat) * per + ctx_tiles + i % lat
        steps = (r // tt // per) * lat
        mod_map = lambda i, ts: (i // lat, 0, 0)
        out_specs = pl.BlockSpec((tt, d), lambda i, ts: (i, 0))
        out_shape = jax.ShapeDtypeStruct((steps * tt, d), F32)
    else:
        tile_of = lambda i: i
        steps = r // tt
        mod_map = lambda i, ts: (jnp.where(i % per < ctx_tiles, 2, i // per), 0, 0)
        out_specs = [pl.BlockSpec((tt, d), lambda i, ts: (i, 0))] * 2
        out_shape = [jax.ShapeDtypeStruct((r, d), F32), jax.ShapeDtypeStruct((r, d), BF16)]
    row = lambda i, ts: (tile_of(i), 0)
    grid_spec = pltpu.PrefetchScalarGridSpec(
        num_scalar_prefetch=1,
        grid=(steps,),
        in_specs=[pl.BlockSpec((tt, d), row), pl.BlockSpec(memory_space=pl.ANY),
                  pl.BlockSpec((tt, 1), row), pl.BlockSpec((tt, 1), row),
                  pl.BlockSpec((1, 6, d), mod_map), pl.BlockSpec((1, 6, d), mod_map),
                  pl.BlockSpec((1, d), const)],
        out_specs=out_specs,
        scratch_shapes=[pltpu.VMEM((2, smax + COMBINE_CHUNK, d), F32), pltpu.VMEM((tt, d), F32),
                        pltpu.SemaphoreType.DMA((2,))],
    )
    return pl.pallas_call(
        functools.partial(_combine_kernel, final=final, smax=smax, tile_of=tile_of),
        grid_spec=grid_spec,
        out_shape=out_shape,
        compiler_params=_cparams(("arbitrary",)),
        name="moe_combine",
    )(tile_start, h, y, start, count, mods, mods_next, g.reshape(1, d))


def _route(aff, b, l, tc, with_ctx):
    aff3 = aff.reshape(b, l, N_EXPERTS)
    ne = N_EXPERTS
    ids, gates = [], []
    sel = jnp.zeros((b, l, ne), jnp.bool_)
    segs = [(tc, l - tc)] + ([(0, tc)] if with_ctx else [])
    for start, n in segs:
        cap = EC_CAPACITY * n // ne
        seg_aff = jnp.swapaxes(aff3[:, start:start + n], 1, 2)
        _, idx = lax.top_k(seg_aff, cap)
        idx = jnp.sort(idx, axis=-1)
        g = jnp.take_along_axis(seg_aff, idx, axis=-1)
        hit = jnp.any(idx[..., None] == jnp.arange(n, dtype=idx.dtype), axis=2)
        sel = sel.at[:, start:start + n].set(jnp.swapaxes(hit, 1, 2))
        rows = idx + start + (jnp.arange(b, dtype=jnp.int32) * l)[:, None, None]
        ids.append(jnp.swapaxes(rows, 0, 1).reshape(ne, b * cap))
        gates.append(jnp.swapaxes(g, 0, 1).reshape(ne, b * cap))
    idx_all = jnp.concatenate(ids, 1).astype(jnp.int32)
    gate_all = jnp.concatenate(gates, 1)[..., None]
    sel_i = sel.reshape(b * l, ne).astype(jnp.int32)
    count = jnp.sum(sel_i, axis=1)
    start_row = jnp.cumsum(count) - count
    slot = start_row[:, None] + jnp.cumsum(sel_i, axis=1) - sel_i
    dst_all = jnp.take_along_axis(slot.T, idx_all, axis=1).astype(jnp.int32)
    tile_n = jnp.sum(count.reshape(-1, COMBINE_TILE), axis=1)
    tile_info = jnp.stack([start_row[::COMBINE_TILE], tile_n]).astype(jnp.int32)
    return idx_all, gate_all, dst_all, tile_info, start_row[:, None].astype(jnp.int32), \
        count[:, None].astype(jnp.int32)


def kernel(x, c, ctx, c_ctx, w_mod, b_mod, norm1_g, norm2_g, w_in, gdn_conv_w, gdn_a_log, gdn_dt_bias,
           gdn_norm_g, sc_conv_w, diff_lambda, diff_norm_g, w_out, w_router, w_e_gate, w_e_up, w_e_down,
           final_norm_g):
    b, t, d = x.shape
    tc = ctx.shape[1]
    l = tc + t
    r = b * l
    depth = w_mod.shape[0]
    tiles = l // ROW_TILE
    nh = GDN_HEADS

    cond = jnp.concatenate([c, c_ctx[None, :], jnp.zeros((8 - b - 1, d), F32)], 0)
    mods_all = _modulation(cond, w_mod, b_mod)[:, :3].reshape(depth, 3, 6, d)
    cos_t, sin_t = _rope_tables(t, tc)
    a_in = 4 * GDN_W
    w_main = jnp.concatenate([w_in[:, :, :a_in], w_in[:, :, a_in + 4 * nh:]], 2).astype(BF16)
    w_ba = jnp.concatenate([w_in[:, :, a_in:a_in + 4 * nh], jnp.zeros((depth, d, LANE - 4 * nh), F32)],
                           2).astype(BF16)
    w_out_b = w_out.astype(BF16)

    h = jnp.concatenate([ctx, x], 1).reshape(r, d)
    a = _norm_mod(h, norm1_g[0], mods_all[0], tiles)
    out = None
    for li in range(depth):
        with_ctx = li < depth - 1
        lam_init = 0.8 - 0.6 * math.exp(-0.3 * li)
        u = _matmul(a, w_main, li, BF16, r // 8, GDN_W)
        ba = _matmul(a, w_ba, li, F32, r // 8, LANE)
        u3 = u.reshape(b, l, N_MAIN)

        qkv = _gdn_prep(u3, gdn_conv_w[li], tc)
        prm = jnp.zeros((8, LANE), F32)
        prm = prm.at[0, 2 * nh:4 * nh].set(-jnp.exp(gdn_a_log[li].reshape(-1)))
        prm = prm.at[1, 2 * nh:4 * nh].set(gdn_dt_bias[li].reshape(-1))
        o_f, o_b = _gdn(qkv, ba.reshape(b, l, LANE), prm, tc, GDN_CHUNK)
        y_b = _short_conv(u3, sc_conv_w[li], tc)
        q_t, k_r, v_t = _attn_prep(u3, cos_t, sin_t)
        y_c = _diff_attn(q_t, k_r, v_t, diff_lambda[li], diff_norm_g[li], lam_init, tc)

        h, a2, aff = _outproj(o_f.reshape(r, GDN_W), o_b.reshape(r, GDN_W), u, y_b.reshape(r, SC_W),
                              y_c.reshape(r, DIFF_W), w_out_b, li, h, mods_all[li],
                              gdn_norm_g[li], norm2_g[li], w_router[li], tiles)
        idx, gate, dst, tile_start, start_row, count = _route(aff, b, l, tc, with_ctx)
        y = _moe(idx, dst, a2, gate, w_e_gate, w_e_up, w_e_down, li)
        if with_ctx:
            h, a = _combine(h, y, tile_start, start_row, count, mods_all[li], mods_all[li + 1],
                            norm1_g[li + 1], tiles, tc, final=False)
        else:
            out = _combine(h, y, tile_start, start_row, count, mods_all[li], mods_all[li],
                           final_norm_g, tiles, tc, final=True)
    return out.reshape(b, t, d)
```
